```python
import jax, jax.numpy as jnp
from jax import lax
import numpy as np

D_MODEL = 1024
BATCH = 8
SEQ = 4096
DEPTH = 4

GRID_W = 64
CTX_LEN = 256
CONV_DIM = 512
CONV_WIDTH = 31
ATT_HEADS = 8
ATT_KV_HEADS = 2
GROUP = ATT_HEADS // ATT_KV_HEADS
HEAD_DIM = 64
ATT_DIM = ATT_HEADS * HEAD_DIM
KV_DIM = ATT_KV_HEADS * HEAD_DIM
WINDOW = 128
BLOCK = 128
ATT_SCALE = HEAD_DIM ** -0.5
ROPE_BASE = 10000.0
EVEN_IN = 2 * CONV_DIM + ATT_DIM + 2 * KV_DIM
EVEN_SPLITS = (CONV_DIM, 2 * CONV_DIM, 2 * CONV_DIM + ATT_DIM, 2 * CONV_DIM + ATT_DIM + KV_DIM)
KV_OFF = 2 * CONV_DIM + ATT_DIM
MIX_EVEN = CONV_DIM + ATT_DIM
RET_HEADS = 4
RET_DK = 256
RET_DV = 512
RET_QK = RET_HEADS * RET_DK
RET_V = RET_HEADS * RET_DV
RET_IN = 2 * RET_QK + 2 * RET_V
RET_SPLITS = (RET_QK, 2 * RET_QK, 2 * RET_QK + RET_V)
RET_CHUNK = 128
D_FF = 2816
N_EVEN = (DEPTH + 1) // 2
N_ODD = DEPTH // 2
DEEPNORM_ALPHA = (2 * DEPTH) ** 0.25
DEEPNORM_BETA = (8 * DEPTH) ** -0.25
LN_EPS = 1e-5
NEG_INF = -1e30

kernel_name = "hybrid_conv_swa_retention_dit"


def _standardize(x):
    xf = x.astype(jnp.float32)
    mu = jnp.mean(xf, axis=-1, keepdims=True)
    var = jnp.mean(jnp.square(xf - mu), axis=-1, keepdims=True)
    return (xf - mu) * lax.rsqrt(var + LN_EPS)


def layer_norm(x, g, b):
    return (_standardize(x) * g.astype(jnp.float32) + b.astype(jnp.float32)).astype(x.dtype)


def modulate(h, shift, scale):
    return h * (1.0 + scale) + shift


def post_norm(h, out, g, b):
    return layer_norm(DEEPNORM_ALPHA * h + out, g, b)


def swiglu(u, w_in, w_out):
    gate, up = jnp.split(u @ w_in, 2, axis=-1)
    return (jax.nn.silu(gate) * up) @ w_out


def ffn_half(h, mm, w_in, w_out, g, b):
    u = modulate(h, mm[0], mm[1])
    return post_norm(h, 0.5 * mm[2] * swiglu(u, w_in, w_out), g, b)


def rotate(x, cos, sin):
    half = x.shape[-1] // 2
    x1, x2 = x[..., :half], x[..., half:]
    cos = cos[:, None, :].astype(x.dtype)
    sin = sin[:, None, :].astype(x.dtype)
    return jnp.concatenate([x1 * cos - x2 * sin, x2 * cos + x1 * sin], axis=-1)


def axial_rope_tables(n_tokens):
    rows = n_tokens // GRID_W
    r = jnp.broadcast_to(jnp.arange(rows, dtype=jnp.float32)[:, None], (rows, GRID_W)).reshape(-1)
    cidx = jnp.broadcast_to(jnp.arange(GRID_W, dtype=jnp.float32)[None, :], (rows, GRID_W)).reshape(-1)
    nf = HEAD_DIM // 4
    inv = ROPE_BASE ** (-jnp.arange(nf, dtype=jnp.float32) / nf)
    ang = jnp.concatenate([r[:, None] * inv[None], cidx[:, None] * inv[None]], axis=-1)
    return jnp.cos(ang), jnp.sin(ang)


def retention_rope_tables(n_tokens):
    inv = 1.0 / (ROPE_BASE ** jnp.linspace(0.0, 1.0, RET_DK // 2, dtype=jnp.float32))
    ang = jnp.arange(n_tokens, dtype=jnp.float32)[:, None] * inv[None]
    return jnp.cos(ang), jnp.sin(ang)


def conformer_conv(a, gt, conv_w, conv_b, g, b):
    y = a * jax.nn.sigmoid(gt)
    y = lax.conv_general_dilated(
        y, conv_w[:, None, :].astype(y.dtype), window_strides=(1,),
        padding=[(CONV_WIDTH // 2, CONV_WIDTH // 2)],
        dimension_numbers=("NWC", "WIO", "NWC"), feature_group_count=CONV_DIM) + conv_b
    return jax.nn.silu(layer_norm(y, g, b))


def _softmax_with_sink(s, sink):
    col = jnp.broadcast_to(sink.astype(jnp.float32).reshape(ATT_KV_HEADS, GROUP, 1, 1), s.shape[:-1] + (1,))
    p = jax.nn.softmax(jnp.concatenate([s, col], axis=-1), axis=-1)
    return p[..., :-1]


def banded_window_attention(q, k, v, kc, vc, sink):
    Bq, L = q.shape[:2]
    nb = L // BLOCK
    qb = q.reshape(Bq, nb, BLOCK, ATT_KV_HEADS, GROUP, HEAD_DIM)

    def band(t):
        tp = jnp.pad(t, ((0, 0), (BLOCK, BLOCK), (0, 0), (0, 0))).reshape(Bq, nb + 2, BLOCK, ATT_KV_HEADS, HEAD_DIM)
        return jnp.concatenate([tp[:, :-2], tp[:, 1:-1], tp[:, 2:]], axis=2)

    kb, vb = band(k), band(v)
    s_loc = jnp.einsum("bnqhgd,bnkhd->bnhgqk", qb, kb).astype(jnp.float32) * ATT_SCALE
    s_ctx = jnp.einsum("bnqhgd,bchd->bnhgqc", qb, kc).astype(jnp.float32) * ATT_SCALE
    blk = jnp.arange(nb)[:, None, None]
    qpos = blk * BLOCK + jnp.arange(BLOCK)[None, :, None]
    kpos = (blk - 1) * BLOCK + jnp.arange(3 * BLOCK)[None, None, :]
    valid = (jnp.abs(kpos - qpos) <= WINDOW) & (kpos >= 0) & (kpos < L)
    s_loc = jnp.where(valid[None, :, None, None], s_loc, NEG_INF)
    p = _softmax_with_sink(jnp.concatenate([s_loc, s_ctx], axis=-1), sink).astype(v.dtype)
    nk = 3 * BLOCK
    o = (jnp.einsum("bnhgqk,bnkhd->bnqhgd", p[..., :nk], vb)
         + jnp.einsum("bnhgqc,bchd->bnqhgd", p[..., nk:], vc))
    return o.reshape(Bq, L, ATT_DIM)


def context_attention(qc, kc, vc, sink):
    Bq, C = qc.shape[:2]
    qg = qc.reshape(Bq, C, ATT_KV_HEADS, GROUP, HEAD_DIM)
    s = jnp.einsum("bqhgd,bkhd->bhgqk", qg, kc).astype(jnp.float32) * ATT_SCALE
    p = _softmax_with_sink(s, sink).astype(vc.dtype)
    return jnp.einsum("bhgqk,bkhd->bqhgd", p, vc).reshape(Bq, C, ATT_DIM)


def even_mixer(u_ctx, u_lat, w_in, conv_w, conv_b, cn_g, cn_b, sink, w_out, cos, sin, ctx_out):
    Bq, L = u_lat.shape[:2]
    C = u_ctx.shape[1]
    a, gt, q, k, v = jnp.split(u_lat @ w_in, EVEN_SPLITS, axis=-1)
    q = rotate(q.reshape(Bq, L, ATT_HEADS, HEAD_DIM), cos, sin)
    k = rotate(k.reshape(Bq, L, ATT_KV_HEADS, HEAD_DIM), cos, sin)
    v = v.reshape(Bq, L, ATT_KV_HEADS, HEAD_DIM)
    if ctx_out:
        ac, gtc, qc, kc, vc = jnp.split(u_ctx @ w_in, EVEN_SPLITS, axis=-1)
    else:
        kc, vc = jnp.split(u_ctx @ w_in[:, KV_OFF:], 2, axis=-1)
    kc = kc.reshape(Bq, C, ATT_KV_HEADS, HEAD_DIM)
    vc = vc.reshape(Bq, C, ATT_KV_HEADS, HEAD_DIM)
    y_lat = jnp.concatenate([conformer_conv(a, gt, conv_w, conv_b, cn_g, cn_b),
                             banded_window_attention(q, k, v, kc, vc, sink)], axis=-1) @ w_out
    y_ctx = None
    if ctx_out:
        y_ctx = jnp.concatenate([conformer_conv(ac, gtc, conv_w, conv_b, cn_g, cn_b),
                                 context_attention(qc, kc, vc, sink)], axis=-1) @ w_out
    return y_ctx, y_lat


def retention_context_state(k, v, log_g):
    L = k.shape[1]
    w = jnp.exp(log_g[None, :] * (L - 1.0 - jnp.arange(L, dtype=jnp.float32))[:, None])
    return jnp.einsum("blhd,blhe->bhde", k.astype(jnp.float32) * w[None, :, :, None], v.astype(jnp.float32))


def retention_chunk_scan(q, k, v, log_g, state0, include_diag):
    Bq, L = q.shape[:2]
    nc = L // RET_CHUNK
    pos = jnp.arange(RET_CHUNK, dtype=jnp.float32)
    rel = pos[:, None] - pos[None, :]
    keep = (rel >= 0) if include_diag else (rel > 0)
    dmat = jnp.where(keep[None], jnp.exp(log_g[:, None, None] * jnp.where(keep, rel, 0.0)[None]), 0.0)
    q_dec = jnp.exp(log_g[None, :] * (pos + 1.0)[:, None])
    k_dec = jnp.exp(log_g[None, :] * (RET_CHUNK - 1.0 - pos)[:, None])
    c_dec = jnp.exp(log_g * RET_CHUNK)

    def chunks(t):
        return jnp.swapaxes(t.astype(jnp.float32).reshape(Bq, nc, RET_CHUNK, *t.shape[2:]), 0, 1)

    def step(state, qkv):
        qi, ki, vi = qkv
        s = jnp.einsum("bqhd,bkhd->bhqk", qi, ki) * dmat
        o = (jnp.einsum("bhqk,bkhe->bqhe", s, vi)
             + jnp.einsum("bqhd,bhde->bqhe", qi * q_dec[None, :, :, None], state))
        state = state * c_dec[None, :, None, None] + jnp.einsum("bkhd,bkhe->bhde", ki * k_dec[None, :, :, None], vi)
        return state, o

    _, o = lax.scan(step, state0, (chunks(q), chunks(k), chunks(v)))
    return jnp.swapaxes(o, 0, 1).reshape(Bq, L, RET_HEADS, RET_DV)


def retention_output(o, g, w_out):
    Bq, L = o.shape[:2]
    o = _standardize(o).reshape(Bq, L, RET_V).astype(g.dtype)
    return (jax.nn.silu(g) * o) @ w_out


def _flip(t):
    return jnp.flip(t, axis=1)


def _bidirectional(q, k, v, log_g, st_f, st_b):
    fwd = retention_chunk_scan(q, k, v, log_g[0], st_f, True)
    bwd = retention_chunk_scan(_flip(q), _flip(k), _flip(v), log_g[1], st_b, False)
    return fwd + _flip(bwd)


def retention_mixer(u_ctx, u_lat, w_in, decay_logit, w_out, cos, sin, ctx_out):
    log_g = jax.nn.log_sigmoid(decay_logit.astype(jnp.float32))
    Bq, L = u_lat.shape[:2]
    C = u_ctx.shape[1]
    scale = RET_DK ** -0.5
    q, k, v, g = jnp.split(u_lat @ w_in, RET_SPLITS, axis=-1)
    q = rotate(q.reshape(Bq, L, RET_HEADS, RET_DK), cos, sin) * scale
    k = rotate(k.reshape(Bq, L, RET_HEADS, RET_DK), cos, sin)
    v = v.reshape(Bq, L, RET_HEADS, RET_DV)
    if ctx_out:
        qc, kc, vc, gc = jnp.split(u_ctx @ w_in, RET_SPLITS, axis=-1)
    else:
        kc, vc = jnp.split(u_ctx @ w_in[:, RET_QK:2 * RET_QK + RET_V], [RET_QK], axis=-1)
    kc = kc.reshape(Bq, C, RET_HEADS, RET_DK)
    vc = vc.reshape(Bq, C, RET_HEADS, RET_DV)
    st_f = retention_context_state(kc, vc, log_g[0])
    st_b = retention_context_state(_flip(kc), _flip(vc), log_g[1])
    y_lat = retention_output(_bidirectional(q, k, v, log_g, st_f, st_b), g, w_out)
    y_ctx = None
    if ctx_out:
        qc = qc.reshape(Bq, C, RET_HEADS, RET_DK) * scale
        zero = jnp.zeros_like(st_f)
        y_ctx = retention_output(_bidirectional(qc, kc, vc, log_g, zero, zero), gc, w_out)
    return y_ctx, y_lat


def setup_inputs(seed: int = 0) -> dict:
    key = jax.random.key(seed)
    ks = jax.random.split(key, 20)
    f = jnp.float32

    def nrm(k, shape, s):
        return jax.random.normal(k, shape, f) * s

    g0 = 1.0 - 2.0 ** (-5.0 - np.arange(RET_HEADS, dtype=np.float32))
    logit0 = jnp.asarray(np.log(g0 / (1.0 - g0)), f)
    return {
        "x": nrm(ks[0], (BATCH, SEQ, D_MODEL), 1.0),
        "c": nrm(ks[1], (BATCH, D_MODEL), 1.0),
        "ctx": nrm(ks[2], (BATCH, CTX_LEN, D_MODEL), 1.0),
        "c_ctx": nrm(ks[3], (D_MODEL,), 1.0),
        "w_ada": nrm(ks[4], (DEPTH, D_MODEL, 9 * D_MODEL), 0.5 * D_MODEL ** -0.5),
        "b_ada": nrm(ks[5], (DEPTH, 9 * D_MODEL), 0.02),
        "ln_g": 1.0 + nrm(ks[6], (DEPTH, 3, D_MODEL), 0.02),
        "ln_b": nrm(ks[7], (DEPTH, 3, D_MODEL), 0.02),
        "ffn_w_in": nrm(ks[8], (DEPTH, 2, D_MODEL, 2 * D_FF), D_MODEL ** -0.5),
        "ffn_w_out": nrm(ks[9], (DEPTH, 2, D_FF, D_MODEL), DEEPNORM_BETA * D_FF ** -0.5),
        "ev_w_in": nrm(ks[10], (N_EVEN, D_MODEL, EVEN_IN), D_MODEL ** -0.5),
        "ev_conv_w": nrm(ks[11], (N_EVEN, CONV_WIDTH, CONV_DIM), CONV_WIDTH ** -0.5),
        "ev_conv_b": nrm(ks[12], (N_EVEN, CONV_DIM), 0.02),
        "ev_norm_g": 1.0 + nrm(ks[13], (N_EVEN, CONV_DIM), 0.02),
        "ev_norm_b": nrm(ks[14], (N_EVEN, CONV_DIM), 0.02),
        "ev_sink": nrm(ks[15], (N_EVEN, ATT_HEADS), 0.5),
        "ev_w_out": nrm(ks[16], (N_EVEN, MIX_EVEN, D_MODEL), DEEPNORM_BETA * MIX_EVEN ** -0.5),
        "ret_w_in": nrm(ks[17], (N_ODD, D_MODEL, RET_IN), D_MODEL ** -0.5),
        "ret_decay": logit0[None, None, :] + nrm(ks[18], (N_ODD, 2, RET_HEADS), 0.1),
        "ret_w_out": nrm(ks[19], (N_ODD, RET_V, D_MODEL), DEEPNORM_BETA * RET_V ** -0.5),
    }


def reference(x, c, ctx, c_ctx, w_ada, b_ada, ln_g, ln_b, ffn_w_in, ffn_w_out,
              ev_w_in, ev_conv_w, ev_conv_b, ev_norm_g, ev_norm_b, ev_sink, ev_w_out,
              ret_w_in, ret_decay, ret_w_out):
    n_tok = x.shape[1]
    cos_a, sin_a = axial_rope_tables(n_tok)
    cos_r, sin_r = retention_rope_tables(n_tok)
    sc = jax.nn.silu(c)
    scc = jax.nn.silu(c_ctx)
    h, hc = x, ctx
    for i in range(DEPTH):
        last = i == DEPTH - 1
        m_lat = jnp.split((sc @ w_ada[i] + b_ada[i])[:, None, :], 9, axis=-1)
        m_ctx = jnp.split((scc @ w_ada[i] + b_ada[i])[None, None, :], 9, axis=-1)
        h = ffn_half(h, m_lat[0:3], ffn_w_in[i, 0], ffn_w_out[i, 0], ln_g[i, 0], ln_b[i, 0])
        hc = ffn_half(hc, m_ctx[0:3], ffn_w_in[i, 0], ffn_w_out[i, 0], ln_g[i, 0], ln_b[i, 0])
        u_lat = modulate(h, m_lat[3], m_lat[4])
        u_ctx = modulate(hc, m_ctx[3], m_ctx[4])
        j = i // 2
        if i % 2 == 0:
            y_ctx, y_lat = even_mixer(u_ctx, u_lat, ev_w_in[j], ev_conv_w[j], ev_conv_b[j], ev_norm_g[j],
                                      ev_norm_b[j], ev_sink[j], ev_w_out[j], cos_a, sin_a, not last)
        else:
            y_ctx, y_lat = retention_mixer(u_ctx, u_lat, ret_w_in[j], ret_decay[j], ret_w_out[j],
                                           cos_r, sin_r, not last)
        h = post_norm(h, m_lat[5] * y_lat, ln_g[i, 1], ln_b[i, 1])
        h = ffn_half(h, m_lat[6:9], ffn_w_in[i, 1], ffn_w_out[i, 1], ln_g[i, 2], ln_b[i, 2])
        if not last:
            hc = post_norm(hc, m_ctx[5] * y_ctx, ln_g[i, 1], ln_b[i, 1])
            hc = ffn_half(hc, m_ctx[6:9], ffn_w_in[i, 1], ffn_w_out[i, 1], ln_g[i, 2], ln_b[i, 2])
    return h
```

```python
import functools

import jax
import jax.numpy as jnp
from jax import lax
from jax.experimental import pallas as pl
from jax.experimental.pallas import tpu as pltpu

F32 = jnp.float32
BF16 = jnp.bfloat16

D_MODEL = 1024
DEPTH = 4
GRID_W = 64
CTX_LEN = 256
CONV_DIM = 512
CONV_WIDTH = 31
CONV_HALO = 16
ATT_HEADS = 8
ATT_KV_HEADS = 2
HEAD_DIM = 64
ATT_DIM = ATT_HEADS * HEAD_DIM
KV_DIM = ATT_KV_HEADS * HEAD_DIM
WINDOW = 128
BLOCK = 128
ATT_SCALE = HEAD_DIM ** -0.5
ROPE_BASE = 10000.0
KV_OFF = 2 * CONV_DIM + ATT_DIM
MIX_EVEN = CONV_DIM + ATT_DIM
RET_HEADS = 4
RET_DK = 256
RET_DV = 512
RET_QK = RET_HEADS * RET_DK
RET_V = RET_HEADS * RET_DV
RET_CHUNK = 128
D_FF = 2816
DEEPNORM_ALPHA = (2 * DEPTH) ** 0.25
LN_EPS = 1e-5
NEG_INF = -1e30

LANES = 128
MOD_ROWS = 16
VMEM_LIMIT_BYTES = 56 * 1024 * 1024
FFN_CHUNK = 256


def _params(*sem):
    return pltpu.CompilerParams(dimension_semantics=sem, vmem_limit_bytes=VMEM_LIMIT_BYTES)


def _dot(a, b):
    return jnp.dot(a, b, preferred_element_type=F32)


def _dot_nt(a, b):
    return lax.dot_general(a, b, (((1,), (1,)), ((), ())), preferred_element_type=F32)


def _dot_tn(a, b):
    return lax.dot_general(a, b, (((0,), (0,)), ((), ())), preferred_element_type=F32)


def _silu(x):
    return x * jax.nn.sigmoid(x)


def _standardize(x):
    mu = jnp.mean(x, axis=-1, keepdims=True)
    d = x - mu
    var = jnp.mean(d * d, axis=-1, keepdims=True)
    return d * lax.rsqrt(var + LN_EPS)


def _post_norm(h, out, g, b):
    return _standardize(DEEPNORM_ALPHA * h + out) * g + b


def _resident(shape, index_map):
    return pl.BlockSpec(shape, index_map, pipeline_mode=pl.Buffered(1))


def _mod_kernel(c_ref, w_ref, b_ref, o_ref):
    s = _silu(c_ref[...]).astype(BF16)
    o_ref[0] = _dot(s, w_ref[0].astype(BF16)) + b_ref[0]


def _modulation(cvec, w_ada, b_ada):
    tn = 1536
    n_out = 9 * D_MODEL
    out = pl.pallas_call(
        _mod_kernel,
        grid=(DEPTH, n_out // tn),
        in_specs=[
            pl.BlockSpec((MOD_ROWS, D_MODEL), lambda i, n: (0, 0)),
            pl.BlockSpec((1, D_MODEL, tn), lambda i, n: (i, 0, n)),
            pl.BlockSpec((1, 1, tn), lambda i, n: (i, 0, n)),
        ],
        out_specs=pl.BlockSpec((1, MOD_ROWS, tn), lambda i, n: (i, 0, n)),
        out_shape=jax.ShapeDtypeStruct((DEPTH, MOD_ROWS, n_out), F32),
        compiler_params=_params("parallel", "parallel"),
        name="adaln_modulation",
    )(cvec, w_ada, b_ada.reshape(DEPTH, 1, n_out))
    return out.reshape(DEPTH, MOD_ROWS, 9, D_MODEL)


def _mod_spec(layer, row_of_batch):
    return pl.BlockSpec((None, None, 9, D_MODEL), lambda b, t: (layer, row_of_batch(b), 0, 0))


def _ln_spec(layer, j):
    return pl.BlockSpec((None, 1, D_MODEL), lambda b, t: (layer * 3 + j, 0, 0))


def _ffn_kernel(h_ref, mod_ref, win_ref, wout_ref, g_ref, b_ref, o_ref, act_ref, *, k0):
    h = h_ref[...]
    shift, scale, gate = mod_ref[k0:k0 + 1], mod_ref[k0 + 1:k0 + 2], mod_ref[k0 + 2:k0 + 3]
    u = (h * (1.0 + scale) + shift).astype(BF16)
    for n in range(D_FF // FFN_CHUNK):
        lo = n * FFN_CHUNK
        gt = _dot(u, win_ref[:, lo:lo + FFN_CHUNK])
        up = _dot(u, win_ref[:, D_FF + lo:D_FF + lo + FFN_CHUNK])
        act_ref[:, lo:lo + FFN_CHUNK] = (_silu(gt) * up).astype(BF16)
    y = _dot(act_ref[...], wout_ref[...])
    o_ref[...] = _post_norm(h, (0.5 * gate) * y, g_ref[...], b_ref[...])


def _ffn_half(h, mod, row_of_batch, w_in, w_out, ln_g, ln_b, layer, j, k0, tm):
    bsz, n, _ = h.shape
    return pl.pallas_call(
        functools.partial(_ffn_kernel, k0=k0),
        grid=(bsz, n // tm),
        in_specs=[
            pl.BlockSpec((None, tm, D_MODEL), lambda b, t: (b, t, 0)),
            _mod_spec(layer, row_of_batch),
            _resident((None, None, D_MODEL, 2 * D_FF), lambda b, t: (layer, j, 0, 0)),
            _resident((None, None, D_FF, D_MODEL), lambda b, t: (layer, j, 0, 0)),
            _ln_spec(layer, 2 * j),
            _ln_spec(layer, 2 * j),
        ],
        out_specs=pl.BlockSpec((None, tm, D_MODEL), lambda b, t: (b, t, 0)),
        out_shape=jax.ShapeDtypeStruct(h.shape, F32),
        scratch_shapes=[pltpu.VMEM((tm, D_FF), BF16)],
        compiler_params=_params("parallel", "parallel"),
        name="ffn_half",
    )(h, mod, w_in, w_out, ln_g, ln_b)


EVEN_COLS = 2 * CONV_DIM + ATT_DIM + 4 * KV_DIM


def _rope64(x, cos, sin):
    width = x.shape[-1]
    reps = width // LANES
    cosw = jnp.concatenate([cos] * reps, axis=1) if reps > 1 else cos
    sinw = jnp.concatenate([sin] * reps, axis=1) if reps > 1 else sin
    lane = lax.broadcasted_iota(jnp.int32, x.shape, 1)
    first_half = (lane % HEAD_DIM) < (HEAD_DIM // 2)
    partner = jnp.where(first_half, pltpu.roll(x, width - HEAD_DIM // 2, 1), pltpu.roll(x, HEAD_DIM // 2, 1))
    return x * cosw + partner * sinw


def _even_proj_kernel(*refs, rope):
    if rope:
        h_ref, mod_ref, w_ref, cos_ref, sin_ref, y_ref, q_ref, k_ref, v_ref = refs
    else:
        h_ref, mod_ref, w_ref, y_ref, q_ref, k_ref, v_ref = refs
    u = (h_ref[...] * (1.0 + mod_ref[4:5]) + mod_ref[3:4]).astype(BF16)
    a = _dot(u, w_ref[:, 0:CONV_DIM])
    gt = _dot(u, w_ref[:, CONV_DIM:2 * CONV_DIM])
    y_ref[...] = a * jax.nn.sigmoid(gt)
    q = _dot(u, w_ref[:, 2 * CONV_DIM:KV_OFF])
    k = _dot(u, w_ref[:, KV_OFF:KV_OFF + 2 * KV_DIM])
    v = _dot(u, w_ref[:, KV_OFF + 2 * KV_DIM:EVEN_COLS])
    if rope:
        cos, sin = cos_ref[...], sin_ref[...]
        q = _rope64(q, cos, sin)
        k = _rope64(k, cos, sin)
    q_ref[...] = (q * ATT_SCALE).astype(BF16)
    k_ref[...] = k.astype(BF16)
    v_ref[...] = v.astype(BF16)


def _even_proj(h, mod, row_of_batch, w, layer, j, rope_tables, tm):
    bsz, n, _ = h.shape
    rope = rope_tables is not None
    in_specs = [
        pl.BlockSpec((None, tm, D_MODEL), lambda b, t: (b, t, 0)),
        _mod_spec(layer, row_of_batch),
        _resident((None, D_MODEL, EVEN_COLS), lambda b, t: (j, 0, 0)),
    ]
    args = [h, mod, w]
    if rope:
        in_specs += [pl.BlockSpec((tm, LANES), lambda b, t: (t, 0))] * 2
        args += list(rope_tables)

    def out(width, dtype):
        return (pl.BlockSpec((None, tm, width), lambda b, t: (b, t, 0)),
                jax.ShapeDtypeStruct((bsz, n, width), dtype))

    outs = [out(CONV_DIM, F32), out(ATT_DIM, BF16), out(2 * KV_DIM, BF16), out(2 * KV_DIM, BF16)]
    return pl.pallas_call(
        functools.partial(_even_proj_kernel, rope=rope),
        grid=(bsz, n // tm),
        in_specs=in_specs,
        out_specs=[o[0] for o in outs],
        out_shape=[o[1] for o in outs],
        compiler_params=_params("parallel", "parallel"),
        name="even_proj",
    )(*args)


CONV_ROWS = 64


def _conv_ln_silu(ybuf_ref, conv_ref, cw_ref, cb_ref, cg_ref, cbeta_ref, tq):
    first = CONV_HALO - CONV_WIDTH // 2
    for c in range(CONV_DIM // LANES):
        cl = slice(c * LANES, (c + 1) * LANES)
        for r0 in range(0, tq, CONV_ROWS):
            acc = jnp.zeros((CONV_ROWS, LANES), F32) + cb_ref[:, cl]
            for j in range(CONV_WIDTH):
                acc = acc + ybuf_ref[r0 + first + j:r0 + first + j + CONV_ROWS, cl] * cw_ref[j:j + 1, cl]
            conv_ref[r0:r0 + CONV_ROWS, cl] = acc
    return _silu(_standardize(conv_ref[...]) * cg_ref[...] + cbeta_ref[...])


def _attend(qe, kloc, vloc, kctx, vctx, sink_col, valid):
    s_ctx = _dot_nt(qe, kctx)
    m = jnp.maximum(jnp.max(s_ctx, axis=-1, keepdims=True), sink_col)
    if kloc is not None:
        s_loc = jnp.where(valid, _dot_nt(qe, kloc), NEG_INF)
        m = jnp.maximum(m, jnp.max(s_loc, axis=-1, keepdims=True))
        p_loc = jnp.exp(s_loc - m)
    p_ctx = jnp.exp(s_ctx - m)
    den = jnp.sum(p_ctx, axis=-1, keepdims=True) + jnp.exp(sink_col - m)
    o = _dot(p_ctx.astype(BF16), vctx)
    if kloc is not None:
        den = den + jnp.sum(p_loc, axis=-1, keepdims=True)
        o = o + _dot(p_loc.astype(BF16), vloc)
    return o / den


def _even_mix_kernel(*refs, latent, tq, n_tiles):
    if latent:
        (sink_ref, h_ref, mod_ref, yprev_ref, ycur_ref, ynext_ref, q_ref,
         kprev_ref, kcur_ref, knext_ref, vprev_ref, vcur_ref, vnext_ref, kctx_ref, vctx_ref,
         cw_ref, cb_ref, cg_ref, cbeta_ref, wout_ref, g_ref, b_ref,
         o_ref, ybuf_ref, conv_ref, kk_ref, vv_ref, cat_ref) = refs
    else:
        (sink_ref, h_ref, mod_ref, ycur_ref, q_ref, kctx_ref, vctx_ref,
         cw_ref, cb_ref, cg_ref, cbeta_ref, wout_ref, g_ref, b_ref,
         o_ref, ybuf_ref, conv_ref, cat_ref) = refs
    i = pl.program_id(1)
    zero_halo = jnp.zeros((CONV_HALO, CONV_DIM), F32)
    if latent:
        ybuf_ref[0:CONV_HALO] = jnp.where(i > 0, yprev_ref[...], zero_halo)
        ybuf_ref[CONV_HALO + tq:] = jnp.where(i < n_tiles - 1, ynext_ref[...], zero_halo)
        kk_ref[0:BLOCK] = kprev_ref[...]
        kk_ref[BLOCK:BLOCK + tq] = kcur_ref[...]
        kk_ref[BLOCK + tq:] = knext_ref[...]
        vv_ref[0:BLOCK] = vprev_ref[...]
        vv_ref[BLOCK:BLOCK + tq] = vcur_ref[...]
        vv_ref[BLOCK + tq:] = vnext_ref[...]
    else:
        ybuf_ref[0:CONV_HALO] = zero_halo
        ybuf_ref[CONV_HALO + tq:] = zero_halo
    ybuf_ref[CONV_HALO:CONV_HALO + tq] = ycur_ref[...]
    cat_ref[:, 0:CONV_DIM] = _conv_ln_silu(ybuf_ref, conv_ref, cw_ref, cb_ref, cg_ref, cbeta_ref, tq).astype(BF16)

    rows = 2 * BLOCK
    lane = lax.broadcasted_iota(jnp.int32, (rows, LANES), 1)
    row = lax.broadcasted_iota(jnp.int32, (rows, 1), 0)
    qrow = lax.broadcasted_iota(jnp.int32, (rows, 3 * BLOCK), 0) % BLOCK
    kcol = lax.broadcasted_iota(jnp.int32, (rows, 3 * BLOCK), 1)
    band = jnp.abs(kcol - BLOCK - qrow) <= WINDOW
    for n in range(tq // BLOCK):
        valid = None
        if latent:
            lo = 0 if n > 0 else jnp.where(i > 0, 0, BLOCK)
            hi = 3 * BLOCK if n < tq // BLOCK - 1 else jnp.where(i < n_tiles - 1, 3 * BLOCK, 2 * BLOCK)
            valid = band & (kcol >= lo) & (kcol < hi)
        for hk in range(ATT_KV_HEADS):
            kvl = slice(hk * LANES, (hk + 1) * LANES)
            q2 = jnp.concatenate(
                [q_ref[n * BLOCK:(n + 1) * BLOCK, (2 * hk + p) * LANES:(2 * hk + p + 1) * LANES] for p in range(2)],
                axis=0)
            kloc = vloc = None
            if latent:
                kloc = kk_ref[n * BLOCK:(n + 3) * BLOCK, kvl]
                vloc = vv_ref[n * BLOCK:(n + 3) * BLOCK, kvl]
            outs = []
            for e in range(2):
                in_e = (lane >= HEAD_DIM) if e else (lane < HEAD_DIM)
                qe = jnp.where(in_e, q2, jnp.zeros_like(q2))
                head0 = hk * 4 + e
                sink_col = jnp.where(row < BLOCK, sink_ref[head0], sink_ref[head0 + 2])
                outs.append(_attend(qe, kloc, vloc, kctx_ref[:, kvl], vctx_ref[:, kvl], sink_col, valid))
            o2 = jnp.where(lane < HEAD_DIM, outs[0], outs[1]).astype(BF16)
            for p in range(2):
                c0 = CONV_DIM + (2 * hk + p) * LANES
                cat_ref[n * BLOCK:(n + 1) * BLOCK, c0:c0 + LANES] = o2[p * BLOCK:(p + 1) * BLOCK]
    y = _dot(cat_ref[...], wout_ref[...])
    o_ref[...] = _post_norm(h_ref[...], mod_ref[5:6] * y, g_ref[...], b_ref[...])


def _even_mix(h, mod, row_of_batch, y, q, k, v, kctx, vctx, sink, cw, cb, cg, cbeta, w_out, ln_g, ln_b,
              layer, j, latent, tq):
    bsz, n, _ = h.shape
    n_tiles = n // tq
    hb = tq // CONV_HALO
    kb = tq // BLOCK

    def tile(width):
        return pl.BlockSpec((None, tq, width), lambda b, t: (b, t, 0))

    def prev(rows, per_tile, width):
        return pl.BlockSpec((None, rows, width), lambda b, t: (b, jnp.maximum(t * per_tile - 1, 0), 0))

    def nxt(rows, per_tile, width):
        last = n // rows - 1
        return pl.BlockSpec((None, rows, width), lambda b, t: (b, jnp.minimum((t + 1) * per_tile, last), 0))

    ctx_kv = pl.BlockSpec((None, CTX_LEN, 2 * KV_DIM), lambda b, t: (b, 0, 0))
    small = lambda rows: pl.BlockSpec((None, rows, CONV_DIM), lambda b, t: (j, 0, 0))
    in_specs = [pl.BlockSpec(memory_space=pltpu.SMEM), tile(D_MODEL), _mod_spec(layer, row_of_batch)]
    args = [sink, h, mod]
    if latent:
        in_specs += [prev(CONV_HALO, hb, CONV_DIM), tile(CONV_DIM), nxt(CONV_HALO, hb, CONV_DIM), tile(ATT_DIM),
                     prev(BLOCK, kb, 2 * KV_DIM), tile(2 * KV_DIM), nxt(BLOCK, kb, 2 * KV_DIM),
                     prev(BLOCK, kb, 2 * KV_DIM), tile(2 * KV_DIM), nxt(BLOCK, kb, 2 * KV_DIM)]
        args += [y, y, y, q, k, k, k, v, v, v]
    else:
        in_specs += [tile(CONV_DIM), tile(ATT_DIM)]
        args += [y, q]
    in_specs += [ctx_kv, ctx_kv, small(CONV_WIDTH), small(1), small(1), small(1),
                 _resident((None, MIX_EVEN, D_MODEL), lambda b, t: (j, 0, 0)),
                 _ln_spec(layer, 1), _ln_spec(layer, 1)]
    args += [kctx, vctx, cw, cb, cg, cbeta, w_out, ln_g, ln_b]
    scratch = [pltpu.VMEM((tq + 2 * CONV_HALO, CONV_DIM), F32), pltpu.VMEM((tq, CONV_DIM), F32)]
    if latent:
        scratch += [pltpu.VMEM((tq + 2 * BLOCK, 2 * KV_DIM), BF16)] * 2
    scratch += [pltpu.VMEM((tq, MIX_EVEN), BF16)]
    return pl.pallas_call(
        functools.partial(_even_mix_kernel, latent=latent, tq=tq, n_tiles=n_tiles),
        grid=(bsz, n_tiles),
        in_specs=in_specs,
        out_specs=tile(D_MODEL),
        out_shape=jax.ShapeDtypeStruct(h.shape, F32),
        scratch_shapes=scratch,
        compiler_params=_params("parallel", "parallel"),
        name="even_mix",
    )(*args)


RET_COLS = 2 * RET_QK + 2 * RET_V


def _rope256(x, cos, sin):
    half = RET_DK // 2
    parts = []
    for hd in range(RET_HEADS):
        x1 = x[:, hd * RET_DK:hd * RET_DK + half]
        x2 = x[:, hd * RET_DK + half:(hd + 1) * RET_DK]
        parts += [x1 * cos - x2 * sin, x2 * cos + x1 * sin]
    return jnp.concatenate(parts, axis=1)


def _ret_proj_kernel(*refs, rope):
    if rope:
        h_ref, mod_ref, w_ref, cos_ref, sin_ref, q_ref, k_ref, v_ref, sg_ref = refs
    else:
        h_ref, mod_ref, w_ref, q_ref, k_ref, v_ref, sg_ref = refs
    u = (h_ref[...] * (1.0 + mod_ref[4:5]) + mod_ref[3:4]).astype(BF16)
    q = _dot(u, w_ref[:, 0:RET_QK])
    k = _dot(u, w_ref[:, RET_QK:2 * RET_QK])
    if rope:
        cos, sin = cos_ref[...], sin_ref[...]
        q = _rope256(q, cos, sin)
        k = _rope256(k, cos, sin)
    q_ref[...] = (q * (RET_DK ** -0.5)).astype(BF16)
    k_ref[...] = k.astype(BF16)
    v_ref[...] = _dot(u, w_ref[:, 2 * RET_QK:2 * RET_QK + RET_V]).astype(BF16)
    sg_ref[...] = _silu(_dot(u, w_ref[:, 2 * RET_QK + RET_V:RET_COLS])).astype(BF16)


def _ret_proj(h, mod, row_of_batch, w, layer, j, rope_tables, tm):
    bsz, n, _ = h.shape
    rope = rope_tables is not None
    in_specs = [
        pl.BlockSpec((None, tm, D_MODEL), lambda b, t: (b, t, 0)),
        _mod_spec(layer, row_of_batch),
        _resident((None, D_MODEL, RET_COLS), lambda b, t: (j, 0, 0)),
    ]
    args = [h, mod, w]
    if rope:
        in_specs += [pl.BlockSpec((tm, LANES), lambda b, t: (t, 0))] * 2
        args += list(rope_tables)
    widths = [RET_QK, RET_QK, RET_V, RET_V]
    return pl.pallas_call(
        functools.partial(_ret_proj_kernel, rope=rope),
        grid=(bsz, n // tm),
        in_specs=in_specs,
        out_specs=[pl.BlockSpec((None, tm, wd), lambda b, t: (b, t, 0)) for wd in widths],
        out_shape=[jax.ShapeDtypeStruct((bsz, n, wd), BF16) for wd in widths],
        compiler_params=_params("parallel", "parallel"),
        name="ret_proj",
    )(*args)


def _ret_scan_kernel(*refs, backward, has_init, finalize, nc):
    refs = list(refs)
    decay_ref, q_ref, k_ref, v_ref = refs[:4]
    pos = 4
    init_ref = ob_ref = sg_ref = None
    if has_init:
        init_ref = refs[pos]
        pos += 1
    if finalize:
        ob_ref, sg_ref = refs[pos:pos + 2]
        pos += 2
    o_ref, fin_ref, state_ref, dmat_ref, qdec_ref, kdec_ref, cdec_ref = refs[pos:]
    i = pl.program_id(1)
    d = 1 if backward else 0

    @pl.when(i == 0)
    def _():
        prow = lax.broadcasted_iota(jnp.int32, (RET_CHUNK, RET_CHUNK), 0)
        pcol = lax.broadcasted_iota(jnp.int32, (RET_CHUNK, RET_CHUNK), 1)
        rel = (pcol - prow) if backward else (prow - pcol)
        keep = (rel > 0) if backward else (rel >= 0)
        relf = jnp.where(keep, rel, 0).astype(F32)
        posf = lax.broadcasted_iota(jnp.int32, (RET_CHUNK, RET_DK), 0).astype(F32)
        qpow = (RET_CHUNK - posf) if backward else (posf + 1.0)
        kpow = posf if backward else (RET_CHUNK - 1.0 - posf)
        for hd in range(RET_HEADS):
            x = jnp.full((RET_CHUNK, RET_CHUNK), decay_ref[d, hd], F32)
            lg = -jnp.log(1.0 + jnp.exp(-x))
            dmat_ref[hd] = jnp.where(keep, jnp.exp(lg * relf), 0.0)
            lgw = jnp.concatenate([lg] * (RET_DK // RET_CHUNK), axis=1)
            qdec_ref[hd] = jnp.exp(lgw * qpow)
            kdec_ref[hd] = jnp.exp(lgw * kpow)
            cdec_ref[hd] = jnp.exp(lg * float(RET_CHUNK))
            if has_init:
                state_ref[hd] = init_ref[hd]
            else:
                state_ref[hd] = jnp.zeros((RET_DK, RET_DV), F32)

    for hd in range(RET_HEADS):
        qh = q_ref[:, hd * RET_DK:(hd + 1) * RET_DK]
        kh = k_ref[:, hd * RET_DK:(hd + 1) * RET_DK]
        vh = v_ref[:, hd * RET_DV:(hd + 1) * RET_DV]
        st = state_ref[hd]
        s = _dot_nt(qh, kh) * dmat_ref[hd]
        o = _dot(s.astype(BF16), vh) + _dot((qh * qdec_ref[hd]).astype(BF16), st.astype(BF16))
        cdec = jnp.concatenate([cdec_ref[hd, 0:1, :]] * (RET_DV // RET_CHUNK), axis=1)
        state_ref[hd] = st * cdec + _dot_tn((kh * kdec_ref[hd]).astype(BF16), vh)
        cols = slice(hd * RET_DV, (hd + 1) * RET_DV)
        if finalize:
            on = _standardize(o + ob_ref[:, cols])
            o_ref[:, cols] = (sg_ref[:, cols] * on).astype(o_ref.dtype)
        else:
            o_ref[:, cols] = o

    @pl.when(i == nc - 1)
    def _():
        fin_ref[...] = state_ref[...]


def _ret_scan(decay, q, k, v, init, ob, sg, backward):
    bsz, n, _ = q.shape
    nc = n // RET_CHUNK
    finalize = ob is not None
    chunk = (lambda b, t: (b, nc - 1 - t, 0)) if backward else (lambda b, t: (b, t, 0))
    state_spec = pl.BlockSpec((None, RET_HEADS, RET_DK, RET_DV), lambda b, t: (b, 0, 0, 0))
    in_specs = [pl.BlockSpec(memory_space=pltpu.SMEM),
                pl.BlockSpec((None, RET_CHUNK, RET_QK), chunk),
                pl.BlockSpec((None, RET_CHUNK, RET_QK), chunk),
                pl.BlockSpec((None, RET_CHUNK, RET_V), chunk)]
    args = [decay, q, k, v]
    if init is not None:
        in_specs.append(state_spec)
        args.append(init)
    if finalize:
        in_specs += [pl.BlockSpec((None, RET_CHUNK, RET_V), chunk)] * 2
        args += [ob, sg]
    return pl.pallas_call(
        functools.partial(_ret_scan_kernel, backward=backward, has_init=init is not None, finalize=finalize, nc=nc),
        grid=(bsz, nc),
        in_specs=in_specs,
        out_specs=[pl.BlockSpec((None, RET_CHUNK, RET_V), chunk), state_spec],
        out_shape=[jax.ShapeDtypeStruct((bsz, n, RET_V), BF16 if finalize else F32),
                   jax.ShapeDtypeStruct((bsz, RET_HEADS, RET_DK, RET_DV), F32)],
        scratch_shapes=[pltpu.VMEM((RET_HEADS, RET_DK, RET_DV), F32),
                        pltpu.VMEM((RET_HEADS, RET_CHUNK, RET_CHUNK), F32),
                        pltpu.VMEM((RET_HEADS, RET_CHUNK, RET_DK), F32),
                        pltpu.VMEM((RET_HEADS, RET_CHUNK, RET_DK), F32),
                        pltpu.VMEM((RET_HEADS, RET_CHUNK, RET_CHUNK), F32)],
        compiler_params=_params("parallel", "arbitrary"),
        name="ret_scan",
    )(*args)


def _out_norm_kernel(z_ref, h_ref, mod_ref, w_ref, g_ref, b_ref, o_ref):
    y = _dot(z_ref[...], w_ref[...])
    o_ref[...] = _post_norm(h_ref[...], mod_ref[5:6] * y, g_ref[...], b_ref[...])


def _out_norm(z, h, mod, row_of_batch, w, ln_g, ln_b, layer, j, tm):
    bsz, n, kdim = z.shape
    return pl.pallas_call(
        _out_norm_kernel,
        grid=(bsz, n // tm),
        in_specs=[
            pl.BlockSpec((None, tm, kdim), lambda b, t: (b, t, 0)),
            pl.BlockSpec((None, tm, D_MODEL), lambda b, t: (b, t, 0)),
            _mod_spec(layer, row_of_batch),
            _resident((None, kdim, D_MODEL), lambda b, t: (j, 0, 0)),
            _ln_spec(layer, 1), _ln_spec(layer, 1),
        ],
        out_specs=pl.BlockSpec((None, tm, D_MODEL), lambda b, t: (b, t, 0)),
        out_shape=jax.ShapeDtypeStruct(h.shape, F32),
        compiler_params=_params("parallel", "parallel"),
        name="out_norm",
    )(z, h, mod, w, ln_g, ln_b)


def _axial_rope_tables(n_tokens):
    rows = n_tokens // GRID_W
    r = jnp.broadcast_to(jnp.arange(rows, dtype=F32)[:, None], (rows, GRID_W)).reshape(-1)
    cidx = jnp.broadcast_to(jnp.arange(GRID_W, dtype=F32)[None, :], (rows, GRID_W)).reshape(-1)
    nf = HEAD_DIM // 4
    inv = ROPE_BASE ** (-jnp.arange(nf, dtype=F32) / nf)
    ang = jnp.concatenate([r[:, None] * inv[None], cidx[:, None] * inv[None]], axis=-1)
    cos, sin = jnp.cos(ang), jnp.sin(ang)
    reps = LANES // HEAD_DIM
    return jnp.tile(jnp.concatenate([cos, cos], axis=-1), (1, reps)), jnp.tile(jnp.concatenate([-sin, sin], axis=-1), (1, reps))


def _retention_rope_tables(n_tokens):
    inv = 1.0 / (ROPE_BASE ** jnp.linspace(0.0, 1.0, RET_DK // 2, dtype=F32))
    ang = jnp.arange(n_tokens, dtype=F32)[:, None] * inv[None]
    return jnp.cos(ang), jnp.sin(ang)


def _even_weight_layout(ev_w_in):
    def dup(cols):
        heads = [cols[..., hd * HEAD_DIM:(hd + 1) * HEAD_DIM] for hd in range(ATT_KV_HEADS)]
        return jnp.concatenate([t for hd in heads for t in (hd, hd)], axis=-1)

    return jnp.concatenate([ev_w_in[..., :KV_OFF], dup(ev_w_in[..., KV_OFF:KV_OFF + KV_DIM]),
                            dup(ev_w_in[..., KV_OFF + KV_DIM:])], axis=-1).astype(BF16)


def kernel(x, c, ctx, c_ctx, w_ada, b_ada, ln_g, ln_b, ffn_w_in, ffn_w_out, ev_w_in, ev_conv_w, ev_conv_b,
           ev_norm_g, ev_norm_b, ev_sink, ev_w_out, ret_w_in, ret_decay, ret_w_out):
    bsz, n_tok, _ = x.shape
    assert bsz < MOD_ROWS and ctx.shape[1] == CTX_LEN
    rope_a = _axial_rope_tables(n_tok)
    rope_r = _retention_rope_tables(n_tok)
    cvec = jnp.zeros((MOD_ROWS, D_MODEL), F32).at[:bsz].set(c).at[bsz].set(c_ctx)
    mod = _modulation(cvec, w_ada, b_ada)
    lat_row = lambda b: b
    ctx_row = lambda b: bsz

    ffn_in = ffn_w_in.astype(BF16)
    ffn_out = ffn_w_out.astype(BF16)
    ev_in = _even_weight_layout(ev_w_in)
    ev_out = ev_w_out.astype(BF16)
    ret_in = ret_w_in.astype(BF16)
    ret_out = ret_w_out.astype(BF16)
    lng = ln_g.reshape(DEPTH * 3, 1, D_MODEL)
    lnb = ln_b.reshape(DEPTH * 3, 1, D_MODEL)
    conv_b = ev_conv_b[:, None, :]
    conv_g = ev_norm_g[:, None, :]
    conv_beta = ev_norm_b[:, None, :]

    tm_lat = min(512, n_tok)
    tm_ctx = CTX_LEN
    tq_lat = min(256, n_tok)

    def ffn(hh, row, tm, layer, j):
        return _ffn_half(hh, mod, row, ffn_in, ffn_out, lng, lnb, layer, j, 6 * j, tm)

    h, hc = x, ctx
    for i in range(DEPTH):
        last = i == DEPTH - 1
        j = i // 2
        h = ffn(h, lat_row, tm_lat, i, 0)
        hc = ffn(hc, ctx_row, tm_ctx, i, 0)
        if i % 2 == 0:
            yc, qc, kc, vc = _even_proj(hc, mod, ctx_row, ev_in, i, j, None, tm_ctx)
            yl, ql, kl, vl = _even_proj(h, mod, lat_row, ev_in, i, j, rope_a, tm_lat)
            mix = functools.partial(_even_mix, kctx=kc, vctx=vc, sink=ev_sink[j], cw=ev_conv_w, cb=conv_b, cg=conv_g,
                                    cbeta=conv_beta, w_out=ev_out, ln_g=lng, ln_b=lnb, layer=i, j=j)
            h = mix(h, mod, lat_row, yl, ql, kl, vl, latent=True, tq=tq_lat)
            if not last:
                hc = mix(hc, mod, ctx_row, yc, qc, None, None, latent=False, tq=CTX_LEN)
        else:
            decay = ret_decay[j]
            qc, kc, vc, sgc = _ret_proj(hc, mod, ctx_row, ret_in, i, j, None, tm_ctx)
            ql, kl, vl, sgl = _ret_proj(h, mod, lat_row, ret_in, i, j, rope_r, tm_lat)
            obc, st_b = _ret_scan(decay, qc, kc, vc, None, None, None, True)
            zc, st_f = _ret_scan(decay, qc, kc, vc, None, obc, sgc, False)
            obl, _ = _ret_scan(decay, ql, kl, vl, st_b, None, None, True)
            zl, _ = _ret_scan(decay, ql, kl, vl, st_f, obl, sgl, False)
            h = _out_norm(zl, h, mod, lat_row, ret_out, lng, lnb, i, j, tm_lat)
            if not last:
                hc = _out_norm(zc, hc, mod, ctx_row, ret_out, lng, lnb, i, j, tm_ctx)
        h = ffn(h, lat_row, tm_lat, i, 1)
        if not last:
            hc = ffn(hc, ctx_row, tm_ctx, i, 1)
    return h
```

```python
import functools

import jax
import jax.numpy as jnp
from jax import lax
from jax.experimental import pallas as pl
from jax.experimental.pallas import tpu as pltpu

F32 = jnp.float32
BF16 = jnp.bfloat16

D_MODEL = 1024
DEPTH = 4
GRID_W = 64
CTX_LEN = 256
CONV_DIM = 512
CONV_WIDTH = 31
CONV_HALO = 16
ATT_HEADS = 8
ATT_KV_HEADS = 2
HEAD_DIM = 64
ATT_DIM = ATT_HEADS * HEAD_DIM
KV_DIM = ATT_KV_HEADS * HEAD_DIM
WINDOW = 128
BLOCK = 128
ATT_SCALE = HEAD_DIM ** -0.5
ROPE_BASE = 10000.0
KV_OFF = 2 * CONV_DIM + ATT_DIM
MIX_EVEN = CONV_DIM + ATT_DIM
RET_HEADS = 4
RET_DK = 256
RET_DV = 512
RET_QK = RET_HEADS * RET_DK
RET_V = RET_HEADS * RET_DV
RET_CHUNK = 128
D_FF = 2816
DEEPNORM_ALPHA = (2 * DEPTH) ** 0.25
LN_EPS = 1e-5
NEG_INF = -1e30

LANES = 128
MOD_ROWS = 16
VMEM_LIMIT_BYTES = 56 * 1024 * 1024
FFN_CHUNK = 256


def _params(*sem):
    return pltpu.CompilerParams(dimension_semantics=sem, vmem_limit_bytes=VMEM_LIMIT_BYTES)


def _dot(a, b):
    return jnp.dot(a, b, preferred_element_type=F32)


def _dot_nt(a, b):
    return lax.dot_general(a, b, (((1,), (1,)), ((), ())), preferred_element_type=F32)


def _dot_tn(a, b):
    return lax.dot_general(a, b, (((0,), (0,)), ((), ())), preferred_element_type=F32)


def _silu(x):
    return x * jax.nn.sigmoid(x)


def _standardize(x):
    mu = jnp.mean(x, axis=-1, keepdims=True)
    d = x - mu
    var = jnp.mean(d * d, axis=-1, keepdims=True)
    return d * lax.rsqrt(var + LN_EPS)


def _post_norm(h, out, g, b):
    return _standardize(DEEPNORM_ALPHA * h + out) * g + b


def _resident(shape, index_map):
    return pl.BlockSpec(shape, index_map, pipeline_mode=pl.Buffered(1))


def _mod_kernel(c_ref, w_ref, b_ref, o_ref):
    s = _silu(c_ref[...]).astype(BF16)
    o_ref[0] = _dot(s, w_ref[0].astype(BF16)) + b_ref[0]


def _modulation(cvec, w_ada, b_ada):
    tn = 1536
    n_out = 9 * D_MODEL
    out = pl.pallas_call(
        _mod_kernel,
        grid=(DEPTH, n_out // tn),
        in_specs=[
            pl.BlockSpec((MOD_ROWS, D_MODEL), lambda i, n: (0, 0)),
            pl.BlockSpec((1, D_MODEL, tn), lambda i, n: (i, 0, n)),
            pl.BlockSpec((1, 1, tn), lambda i, n: (i, 0, n)),
        ],
        out_specs=pl.BlockSpec((1, MOD_ROWS, tn), lambda i, n: (i, 0, n)),
        out_shape=jax.ShapeDtypeStruct((DEPTH, MOD_ROWS, n_out), F32),
        compiler_params=_params("parallel", "parallel"),
        name="adaln_modulation",
    )(cvec, w_ada, b_ada.reshape(DEPTH, 1, n_out))
    return out.reshape(DEPTH, MOD_ROWS, 9, D_MODEL)


def _mod_spec(layer, row_of_batch):
    return pl.BlockSpec((None, None, 9, D_MODEL), lambda b, t: (layer, row_of_batch(b), 0, 0))


def _ln_spec(layer, j):
    return pl.BlockSpec((None, 1, D_MODEL), lambda b, t: (layer * 3 + j, 0, 0))


def _ffn_kernel(h_ref, mod_ref, win_ref, wout_ref, g_ref, b_ref, o_ref, act_ref, *, k0):
    h = h_ref[...]
    shift, scale, gate = mod_ref[k0:k0 + 1], mod_ref[k0 + 1:k0 + 2], mod_ref[k0 + 2:k0 + 3]
    u = (h * (1.0 + scale) + shift).astype(BF16)
    for n in range(D_FF // FFN_CHUNK):
        lo = n * FFN_CHUNK
        gt = _dot(u, win_ref[:, lo:lo + FFN_CHUNK])
        up = _dot(u, win_ref[:, D_FF + lo:D_FF + lo + FFN_CHUNK])
        act_ref[:, lo:lo + FFN_CHUNK] = (_silu(gt) * up).astype(BF16)
    y = _dot(act_ref[...], wout_ref[...])
    o_ref[...] = _post_norm(h, (0.5 * gate) * y, g_ref[...], b_ref[...])


def _ffn_half(h, mod, row_of_batch, w_in, w_out, ln_g, ln_b, layer, j, k0, tm):
    bsz, n, _ = h.shape
    return pl.pallas_call(
        functools.partial(_ffn_kernel, k0=k0),
        grid=(bsz, n // tm),
        in_specs=[
            pl.BlockSpec((None, tm, D_MODEL), lambda b, t: (b, t, 0)),
            _mod_spec(layer, row_of_batch),
            _resident((None, None, D_MODEL, 2 * D_FF), lambda b, t: (layer, j, 0, 0)),
            _resident((None, None, D_FF, D_MODEL), lambda b, t: (layer, j, 0, 0)),
            _ln_spec(layer, 2 * j),
            _ln_spec(layer, 2 * j),
        ],
        out_specs=pl.BlockSpec((None, tm, D_MODEL), lambda b, t: (b, t, 0)),
        out_shape=jax.ShapeDtypeStruct(h.shape, F32),
        scratch_shapes=[pltpu.VMEM((tm, D_FF), BF16)],
        compiler_params=_params("parallel", "parallel"),
        name="ffn_half",
    )(h, mod, w_in, w_out, ln_g, ln_b)


EVEN_COLS = 2 * CONV_DIM + ATT_DIM + 4 * KV_DIM


def _rope64(x, cos, sin):
    width = x.shape[-1]
    reps = width // LANES
    cosw = jnp.concatenate([cos] * reps, axis=1) if reps > 1 else cos
    sinw = jnp.concatenate([sin] * reps, axis=1) if reps > 1 else sin
    lane = lax.broadcasted_iota(jnp.int32, x.shape, 1)
    first_half = (lane % HEAD_DIM) < (HEAD_DIM // 2)
    partner = jnp.where(first_half, pltpu.roll(x, width - HEAD_DIM // 2, 1), pltpu.roll(x, HEAD_DIM // 2, 1))
    return x * cosw + partner * sinw


def _even_proj_kernel(*refs, rope):
    if rope:
        h_ref, mod_ref, w_ref, cos_ref, sin_ref, y_ref, q_ref, k_ref, v_ref = refs
    else:
        h_ref, mod_ref, w_ref, y_ref, q_ref, k_ref, v_ref = refs
    u = (h_ref[...] * (1.0 + mod_ref[4:5]) + mod_ref[3:4]).astype(BF16)
    a = _dot(u, w_ref[:, 0:CONV_DIM])
    gt = _dot(u, w_ref[:, CONV_DIM:2 * CONV_DIM])
    y_ref[...] = a * jax.nn.sigmoid(gt)
    q = _dot(u, w_ref[:, 2 * CONV_DIM:KV_OFF])
    k = _dot(u, w_ref[:, KV_OFF:KV_OFF + 2 * KV_DIM])
    v = _dot(u, w_ref[:, KV_OFF + 2 * KV_DIM:EVEN_COLS])
    if rope:
        cos, sin = cos_ref[...], sin_ref[...]
        q = _rope64(q, cos, sin)
        k = _rope64(k, cos, sin)
    q_ref[...] = (q * ATT_SCALE).astype(BF16)
    k_ref[...] = k.astype(BF16)
    v_ref[...] = v.astype(BF16)


def _even_proj(h, mod, row_of_batch, w, layer, j, rope_tables, tm):
    bsz, n, _ = h.shape
    rope = rope_tables is not None
    in_specs = [
        pl.BlockSpec((None, tm, D_MODEL), lambda b, t: (b, t, 0)),
        _mod_spec(layer, row_of_batch),
        _resident((None, D_MODEL, EVEN_COLS), lambda b, t: (j, 0, 0)),
    ]
    args = [h, mod, w]
    if rope:
        in_specs += [pl.BlockSpec((tm, LANES), lambda b, t: (t, 0))] * 2
        args += list(rope_tables)

    def out(width, dtype):
        return (pl.BlockSpec((None, tm, width), lambda b, t: (b, t, 0)),
                jax.ShapeDtypeStruct((bsz, n, width), dtype))

    outs = [out(CONV_DIM, F32), out(ATT_DIM, BF16), out(2 * KV_DIM, BF16), out(2 * KV_DIM, BF16)]
    return pl.pallas_call(
        functools.partial(_even_proj_kernel, rope=rope),
        grid=(bsz, n // tm),
        in_specs=in_specs,
        out_specs=[o[0] for o in outs],
        out_shape=[o[1] for o in outs],
        compiler_params=_params("parallel", "parallel"),
        name="even_proj",
    )(*args)


CONV_ROWS = 64


def _conv_ln_silu(ybuf_ref, conv_ref, cw_ref, cb_ref, cg_ref, cbeta_ref, tq):
    first = CONV_HALO - CONV_WIDTH // 2
    for c in range(CONV_DIM // LANES):
        cl = slice(c * LANES, (c + 1) * LANES)
        for r0 in range(0, tq, CONV_ROWS):
            acc = jnp.zeros((CONV_ROWS, LANES), F32) + cb_ref[:, cl]
            for j in range(CONV_WIDTH):
                acc = acc + ybuf_ref[r0 + first + j:r0 + first + j + CONV_ROWS, cl] * cw_ref[j:j + 1, cl]
            conv_ref[r0:r0 + CONV_ROWS, cl] = acc
    return _silu(_standardize(conv_ref[...]) * cg_ref[...] + cbeta_ref[...])


def _attend(qe, kloc, vloc, kctx, vctx, sink_col, valid):
    s_ctx = _dot_nt(qe, kctx)
    m = jnp.maximum(jnp.max(s_ctx, axis=-1, keepdims=True), sink_col)
    if kloc is not None:
        s_loc = jnp.where(valid, _dot_nt(qe, kloc), NEG_INF)
        m = jnp.maximum(m, jnp.max(s_loc, axis=-1, keepdims=True))
        p_loc = jnp.exp(s_loc - m)
    p_ctx = jnp.exp(s_ctx - m)
    den = jnp.sum(p_ctx, axis=-1, keepdims=True) + jnp.exp(sink_col - m)
    o = _dot(p_ctx.astype(BF16), vctx)
    if kloc is not None:
        den = den + jnp.sum(p_loc, axis=-1, keepdims=True)
        o = o + _dot(p_loc.astype(BF16), vloc)
    return o / den


def _even_mix_kernel(*refs, latent, tq, n_tiles):
    if latent:
        (sink_ref, h_ref, mod_ref, yprev_ref, ycur_ref, ynext_ref, q_ref,
         kprev_ref, kcur_ref, knext_ref, vprev_ref, vcur_ref, vnext_ref, kctx_ref, vctx_ref,
         cw_ref, cb_ref, cg_ref, cbeta_ref, wout_ref, g_ref, b_ref,
         o_ref, ybuf_ref, conv_ref, kk_ref, vv_ref, cat_ref) = refs
    else:
        (sink_ref, h_ref, mod_ref, ycur_ref, q_ref, kctx_ref, vctx_ref,
         cw_ref, cb_ref, cg_ref, cbeta_ref, wout_ref, g_ref, b_ref,
         o_ref, ybuf_ref, conv_ref, cat_ref) = refs
    i = pl.program_id(1)
    zero_halo = jnp.zeros((CONV_HALO, CONV_DIM), F32)
    if latent:
        ybuf_ref[0:CONV_HALO] = jnp.where(i > 0, yprev_ref[...], zero_halo)
        ybuf_ref[CONV_HALO + tq:] = jnp.where(i < n_tiles - 1, ynext_ref[...], zero_halo)
        kk_ref[0:BLOCK] = kprev_ref[...]
        kk_ref[BLOCK:BLOCK + tq] = kcur_ref[...]
        kk_ref[BLOCK + tq:] = knext_ref[...]
        vv_ref[0:BLOCK] = vprev_ref[...]
        vv_ref[BLOCK:BLOCK + tq] = vcur_ref[...]
        vv_ref[BLOCK + tq:] = vnext_ref[...]
    else:
        ybuf_ref[0:CONV_HALO] = zero_halo
        ybuf_ref[CONV_HALO + tq:] = zero_halo
    ybuf_ref[CONV_HALO:CONV_HALO + tq] = ycur_ref[...]
    cat_ref[:, 0:CONV_DIM] = _conv_ln_silu(ybuf_ref, conv_ref, cw_ref, cb_ref, cg_ref, cbeta_ref, tq).astype(BF16)

    rows = 2 * BLOCK
    lane = lax.broadcasted_iota(jnp.int32, (rows, LANES), 1)
    row = lax.broadcasted_iota(jnp.int32, (rows, 1), 0)
    qrow = lax.broadcasted_iota(jnp.int32, (rows, 3 * BLOCK), 0) % BLOCK
    kcol = lax.broadcasted_iota(jnp.int32, (rows, 3 * BLOCK), 1)
    band = jnp.abs(kcol - BLOCK - qrow) <= WINDOW
    for n in range(tq // BLOCK):
        valid = None
        if latent:
            lo = 0 if n > 0 else jnp.where(i > 0, 0, BLOCK)
            hi = 3 * BLOCK if n < tq // BLOCK - 1 else jnp.where(i < n_tiles - 1, 3 * BLOCK, 2 * BLOCK)
            valid = band & (kcol >= lo) & (kcol < hi)
        for hk in range(ATT_KV_HEADS):
            kvl = slice(hk * LANES, (hk + 1) * LANES)
            q2 = jnp.concatenate(
                [q_ref[n * BLOCK:(n + 1) * BLOCK, (2 * hk + p) * LANES:(2 * hk + p + 1) * LANES] for p in range(2)],
                axis=0)
            kloc = vloc = None
            if latent:
                kloc = kk_ref[n * BLOCK:(n + 3) * BLOCK, kvl]
                vloc = vv_ref[n * BLOCK:(n + 3) * BLOCK, kvl]
            outs = []
            for e in range(2):
                in_e = (lane >= HEAD_DIM) if e else (lane < HEAD_DIM)
                qe = jnp.where(in_e, q2, jnp.zeros_like(q2))
                head0 = hk * 4 + e
                sink_col = jnp.where(row < BLOCK, sink_ref[head0], sink_ref[head0 + 2])
                outs.append(_attend(qe, kloc, vloc, kctx_ref[:, kvl], vctx_ref[:, kvl], sink_col, valid))
            o2 = jnp.where(lane < HEAD_DIM, outs[0], outs[1]).astype(BF16)
            for p in range(2):
                c0 = CONV_DIM + (2 * hk + p) * LANES
                cat_ref[n * BLOCK:(n + 1) * BLOCK, c0:c0 + LANES] = o2[p * BLOCK:(p + 1) * BLOCK]
    y = _dot(cat_ref[...], wout_ref[...])
    o_ref[...] = _post_norm(h_ref[...], mod_ref[5:6] * y, g_ref[...], b_ref[...])


def _even_mix(h, mod, row_of_batch, y, q, k, v, kctx, vctx, sink, cw, cb, cg, cbeta, w_out, ln_g, ln_b,
              layer, j, latent, tq):
    bsz, n, _ = h.shape
    n_tiles = n // tq
    hb = tq // CONV_HALO
    kb = tq // BLOCK

    def tile(width):
        return pl.BlockSpec((None, tq, width), lambda b, t: (b, t, 0))

    def prev(rows, per_tile, width):
        return pl.BlockSpec((None, rows, width), lambda b, t: (b, jnp.maximum(t * per_tile - 1, 0), 0))

    def nxt(rows, per_tile, width):
        last = n // rows - 1
        return pl.BlockSpec((None, rows, width), lambda b, t: (b, jnp.minimum((t + 1) * per_tile, last), 0))

    ctx_kv = pl.BlockSpec((None, CTX_LEN, 2 * KV_DIM), lambda b, t: (b, 0, 0))
    small = lambda rows: pl.BlockSpec((None, rows, CONV_DIM), lambda b, t: (j, 0, 0))
    in_specs = [pl.BlockSpec(memory_space=pltpu.SMEM), tile(D_MODEL), _mod_spec(layer, row_of_batch)]
    args = [sink, h, mod]
    if latent:
        in_specs += [prev(CONV_HALO, hb, CONV_DIM), tile(CONV_DIM), nxt(CONV_HALO, hb, CONV_DIM), tile(ATT_DIM),
                     prev(BLOCK, kb, 2 * KV_DIM), tile(2 * KV_DIM), nxt(BLOCK, kb, 2 * KV_DIM),
                     prev(BLOCK, kb, 2 * KV_DIM), tile(2 * KV_DIM), nxt(BLOCK, kb, 2 * KV_DIM)]
        args += [y, y, y, q, k, k, k, v, v, v]
    else:
        in_specs += [tile(CONV_DIM), tile(ATT_DIM)]
        args += [y, q]
    in_specs += [ctx_kv, ctx_kv, small(CONV_WIDTH), small(1), small(1), small(1),
                 _resident((None, MIX_EVEN, D_MODEL), lambda b, t: (j, 0, 0)),
                 _ln_spec(layer, 1), _ln_spec(layer, 1)]
    args += [kctx, vctx, cw, cb, cg, cbeta, w_out, ln_g, ln_b]
    scratch = [pltpu.VMEM((tq + 2 * CONV_HALO, CONV_DIM), F32), pltpu.VMEM((tq, CONV_DIM), F32)]
    if latent:
        scratch += [pltpu.VMEM((tq + 2 * BLOCK, 2 * KV_DIM), BF16)] * 2
    scratch += [pltpu.VMEM((tq, MIX_EVEN), BF16)]
    return pl.pallas_call(
        functools.partial(_even_mix_kernel, latent=latent, tq=tq, n_tiles=n_tiles),
        grid=(bsz, n_tiles),
        in_specs=in_specs,
        out_specs=tile(D_MODEL),
        out_shape=jax.ShapeDtypeStruct(h.shape, F32),
        scratch_shapes=scratch,
        compiler_params=_params("parallel", "parallel"),
        name="even_mix",
    )(*args)


RET_COLS = 2 * RET_QK + 2 * RET_V


def _rope256(x, cos, sin):
    half = RET_DK // 2
    parts = []
    for hd in range(RET_HEADS):
        x1 = x[:, hd * RET_DK:hd * RET_DK + half]
        x2 = x[:, hd * RET_DK + half:(hd + 1) * RET_DK]
        parts += [x1 * cos - x2 * sin, x2 * cos + x1 * sin]
    return jnp.concatenate(parts, axis=1)


def _ret_proj_kernel(*refs, rope):
    if rope:
        h_ref, mod_ref, w_ref, cos_ref, sin_ref, q_ref, k_ref, v_ref, sg_ref = refs
    else:
        h_ref, mod_ref, w_ref, q_ref, k_ref, v_ref, sg_ref = refs
    u = (h_ref[...] * (1.0 + mod_ref[4:5]) + mod_ref[3:4]).astype(BF16)
    q = _dot(u, w_ref[:, 0:RET_QK])
    k = _dot(u, w_ref[:, RET_QK:2 * RET_QK])
    if rope:
        cos, sin = cos_ref[...], sin_ref[...]
        q = _rope256(q, cos, sin)
        k = _rope256(k, cos, sin)
    q_ref[...] = (q * (RET_DK ** -0.5)).astype(BF16)
    k_ref[...] = k.astype(BF16)
    v_ref[...] = _dot(u, w_ref[:, 2 * RET_QK:2 * RET_QK + RET_V]).astype(BF16)
    sg_ref[...] = _silu(_dot(u, w_ref[:, 2 * RET_QK + RET_V:RET_COLS])).astype(BF16)


def _ret_proj(h, mod, row_of_batch, w, layer, j, rope_tables, tm):
    bsz, n, _ = h.shape
    rope = rope_tables is not None
    in_specs = [
        pl.BlockSpec((None, tm, D_MODEL), lambda b, t: (b, t, 0)),
        _mod_spec(layer, row_of_batch),
        _resident((None, D_MODEL, RET_COLS), lambda b, t: (j, 0, 0)),
    ]
    args = [h, mod, w]
    if rope:
        in_specs += [pl.BlockSpec((tm, LANES), lambda b, t: (t, 0))] * 2
        args += list(rope_tables)
    widths = [RET_QK, RET_QK, RET_V, RET_V]
    return pl.pallas_call(
        functools.partial(_ret_proj_kernel, rope=rope),
        grid=(bsz, n // tm),
        in_specs=in_specs,
        out_specs=[pl.BlockSpec((None, tm, wd), lambda b, t: (b, t, 0)) for wd in widths],
        out_shape=[jax.ShapeDtypeStruct((bsz, n, wd), BF16) for wd in widths],
        compiler_params=_params("parallel", "parallel"),
        name="ret_proj",
    )(*args)


RET_BLOCK = 256


def _ret_mix_kernel(*refs, nc, has_init, out_state):
    refs = list(refs)
    decay_ref, q_ref, k_ref, v_ref, sg_ref, h_ref, mod_ref = refs[:7]
    pos = 7
    init_ref = fin_ref = None
    if has_init:
        init_ref = refs[pos]
        pos += 1
    w_ref, g_ref, b_ref, o_ref = refs[pos:pos + 4]
    pos += 4
    if out_state:
        fin_ref = refs[pos]
        pos += 1
    state_ref, dmat_ref, qdec_ref, kdec_ref, cdec_ref, ob_ref, z_ref = refs[pos:]
    s = pl.program_id(1)
    rb = RET_BLOCK

    def load_state(d):
        for hd in range(RET_HEADS):
            state_ref[hd] = init_ref[d, hd] if has_init else jnp.zeros((RET_DK, RET_DV), F32)

    @pl.when(s == 0)
    def _():
        prow = lax.broadcasted_iota(jnp.int32, (rb, rb), 0)
        pcol = lax.broadcasted_iota(jnp.int32, (rb, rb), 1)
        posf = lax.broadcasted_iota(jnp.int32, (rb, LANES), 0).astype(F32)
        for d in range(2):
            rel = (pcol - prow) if d else (prow - pcol)
            keep = (rel > 0) if d else (rel >= 0)
            relf = jnp.where(keep, rel, 0).astype(F32)
            qpow = (rb - posf) if d else (posf + 1.0)
            kpow = posf if d else (rb - 1.0 - posf)
            for hd in range(RET_HEADS):
                def log_gamma(shape):
                    return -jnp.log(1.0 + jnp.exp(-jnp.full(shape, decay_ref[d, hd], F32)))

                dmat_ref[d, hd] = jnp.where(keep, jnp.exp(log_gamma((rb, rb)) * relf), 0.0)
                qdec_ref[d, hd] = jnp.exp(log_gamma((rb, LANES)) * qpow)
                kdec_ref[d, hd] = jnp.exp(log_gamma((rb, LANES)) * kpow)
                cdec_ref[d, hd] = jnp.exp(log_gamma((8, LANES)) * float(rb))
        load_state(1)

    @pl.when(s == nc)
    def _():
        if out_state:
            fin_ref[1] = state_ref[...]
        load_state(0)

    def lanes(t, width):
        return jnp.concatenate([t] * (width // LANES), axis=1)

    def head_out(d, hd):
        qh = q_ref[:, hd * RET_DK:(hd + 1) * RET_DK]
        kh = k_ref[:, hd * RET_DK:(hd + 1) * RET_DK]
        vh = v_ref[:, hd * RET_DV:(hd + 1) * RET_DV]
        st = state_ref[hd]
        sc = _dot_nt(qh, kh) * dmat_ref[d, hd]
        o = _dot(sc.astype(BF16), vh) + _dot((qh * lanes(qdec_ref[d, hd], RET_DK)).astype(BF16), st.astype(BF16))
        kd = (kh * lanes(kdec_ref[d, hd], RET_DK)).astype(BF16)
        state_ref[hd] = st * lanes(cdec_ref[d, hd, 0:1, :], RET_DV) + _dot_tn(kd, vh)
        return o

    @pl.when(s < nc)
    def _():
        row0 = pl.multiple_of((nc - 1 - s) * rb, rb)
        for hd in range(RET_HEADS):
            ob_ref[pl.ds(row0, rb), hd * RET_DV:(hd + 1) * RET_DV] = head_out(1, hd).astype(BF16)

    @pl.when(s >= nc)
    def _():
        row0 = pl.multiple_of((s - nc) * rb, rb)
        for hd in range(RET_HEADS):
            cols = slice(hd * RET_DV, (hd + 1) * RET_DV)
            on = _standardize(head_out(0, hd) + ob_ref[pl.ds(row0, rb), cols].astype(F32))
            z_ref[:, cols] = (sg_ref[:, cols] * on).astype(BF16)
        y = _dot(z_ref[...], w_ref[...])
        o_ref[...] = _post_norm(h_ref[...], mod_ref[5:6] * y, g_ref[...], b_ref[...])

    if out_state:
        @pl.when(s == 2 * nc - 1)
        def _():
            fin_ref[0] = state_ref[...]


def _ret_mix(decay, q, k, v, sg, h, mod, row_of_batch, init, w, ln_g, ln_b, layer, j, out_state):
    bsz, n, _ = q.shape
    rb = RET_BLOCK
    nc = n // rb
    chunk = lambda b, s: (b, jnp.where(s < nc, nc - 1 - s, s - nc), 0)
    fwd = lambda b, s: (b, jnp.maximum(s - nc, 0), 0)
    state_shape = (None, 2, RET_HEADS, RET_DK, RET_DV)
    state_index = lambda b, s: (b, 0, 0, 0, 0)
    in_specs = [pl.BlockSpec(memory_space=pltpu.SMEM),
                pl.BlockSpec((None, rb, RET_QK), chunk),
                pl.BlockSpec((None, rb, RET_QK), chunk),
                pl.BlockSpec((None, rb, RET_V), chunk),
                pl.BlockSpec((None, rb, RET_V), fwd),
                pl.BlockSpec((None, rb, D_MODEL), fwd),
                _mod_spec(layer, row_of_batch)]
    args = [decay, q, k, v, sg, h, mod]
    if init is not None:
        in_specs.append(_resident(state_shape, state_index))
        args.append(init)
    in_specs += [_resident((None, RET_V, D_MODEL), lambda b, s: (j, 0, 0)), _ln_spec(layer, 1), _ln_spec(layer, 1)]
    args += [w, ln_g, ln_b]
    out_specs = [pl.BlockSpec((None, rb, D_MODEL), fwd)]
    out_shape = [jax.ShapeDtypeStruct(h.shape, F32)]
    if out_state:
        out_specs.append(pl.BlockSpec(state_shape, state_index))
        out_shape.append(jax.ShapeDtypeStruct((bsz, 2, RET_HEADS, RET_DK, RET_DV), F32))
    return pl.pallas_call(
        functools.partial(_ret_mix_kernel, nc=nc, has_init=init is not None, out_state=out_state),
        grid=(bsz, 2 * nc),
        in_specs=in_specs,
        out_specs=out_specs,
        out_shape=out_shape,
        scratch_shapes=[pltpu.VMEM((RET_HEADS, RET_DK, RET_DV), F32),
                        pltpu.VMEM((2, RET_HEADS, rb, rb), F32),
                        pltpu.VMEM((2, RET_HEADS, rb, LANES), F32),
                        pltpu.VMEM((2, RET_HEADS, rb, LANES), F32),
                        pltpu.VMEM((2, RET_HEADS, 8, LANES), F32),
                        pltpu.VMEM((n, RET_V), BF16),
                        pltpu.VMEM((rb, RET_V), BF16)],
        compiler_params=_params("parallel", "arbitrary"),
        name="ret_mix",
    )(*args)


def _axial_rope_tables(n_tokens):
    rows = n_tokens // GRID_W
    r = jnp.broadcast_to(jnp.arange(rows, dtype=F32)[:, None], (rows, GRID_W)).reshape(-1)
    cidx = jnp.broadcast_to(jnp.arange(GRID_W, dtype=F32)[None, :], (rows, GRID_W)).reshape(-1)
    nf = HEAD_DIM // 4
    inv = ROPE_BASE ** (-jnp.arange(nf, dtype=F32) / nf)
    ang = jnp.concatenate([r[:, None] * inv[None], cidx[:, None] * inv[None]], axis=-1)
    cos, sin = jnp.cos(ang), jnp.sin(ang)
    reps = LANES // HEAD_DIM
    return jnp.tile(jnp.concatenate([cos, cos], axis=-1), (1, reps)), jnp.tile(jnp.concatenate([-sin, sin], axis=-1), (1, reps))


def _retention_rope_tables(n_tokens):
    inv = 1.0 / (ROPE_BASE ** jnp.linspace(0.0, 1.0, RET_DK // 2, dtype=F32))
    ang = jnp.arange(n_tokens, dtype=F32)[:, None] * inv[None]
    return jnp.cos(ang), jnp.sin(ang)


def _even_weight_layout(ev_w_in):
    def dup(cols):
        heads = [cols[..., hd * HEAD_DIM:(hd + 1) * HEAD_DIM] for hd in range(ATT_KV_HEADS)]
        return jnp.concatenate([t for hd in heads for t in (hd, hd)], axis=-1)

    return jnp.concatenate([ev_w_in[..., :KV_OFF], dup(ev_w_in[..., KV_OFF:KV_OFF + KV_DIM]),
                            dup(ev_w_in[..., KV_OFF + KV_DIM:])], axis=-1).astype(BF16)


def kernel(x, c, ctx, c_ctx, w_ada, b_ada, ln_g, ln_b, ffn_w_in, ffn_w_out, ev_w_in, ev_conv_w, ev_conv_b,
           ev_norm_g, ev_norm_b, ev_sink, ev_w_out, ret_w_in, ret_decay, ret_w_out):
    bsz, n_tok, _ = x.shape
    assert bsz < MOD_ROWS and ctx.shape[1] == CTX_LEN
    rope_a = _axial_rope_tables(n_tok)
    rope_r = _retention_rope_tables(n_tok)
    cvec = jnp.zeros((MOD_ROWS, D_MODEL), F32).at[:bsz].set(c).at[bsz].set(c_ctx)
    mod = _modulation(cvec, w_ada, b_ada)
    lat_row = lambda b: b
    ctx_row = lambda b: bsz

    ffn_in = ffn_w_in.astype(BF16)
    ffn_out = ffn_w_out.astype(BF16)
    ev_in = _even_weight_layout(ev_w_in)
    ev_out = ev_w_out.astype(BF16)
    ret_in = ret_w_in.astype(BF16)
    ret_out = ret_w_out.astype(BF16)
    lng = ln_g.reshape(DEPTH * 3, 1, D_MODEL)
    lnb = ln_b.reshape(DEPTH * 3, 1, D_MODEL)
    conv_b = ev_conv_b[:, None, :]
    conv_g = ev_norm_g[:, None, :]
    conv_beta = ev_norm_b[:, None, :]

    tm_lat = min(512, n_tok)
    tm_ctx = CTX_LEN
    tq_lat = min(256, n_tok)

    def ffn(hh, row, tm, layer, j):
        return _ffn_half(hh, mod, row, ffn_in, ffn_out, lng, lnb, layer, j, 6 * j, tm)

    h, hc = x, ctx
    for i in range(DEPTH):
        last = i == DEPTH - 1
        j = i // 2
        h = ffn(h, lat_row, tm_lat, i, 0)
        hc = ffn(hc, ctx_row, tm_ctx, i, 0)
        if i % 2 == 0:
            yc, qc, kc, vc = _even_proj(hc, mod, ctx_row, ev_in, i, j, None, tm_ctx)
            yl, ql, kl, vl = _even_proj(h, mod, lat_row, ev_in, i, j, rope_a, tm_lat)
            mix = functools.partial(_even_mix, kctx=kc, vctx=vc, sink=ev_sink[j], cw=ev_conv_w, cb=conv_b, cg=conv_g,
                                    cbeta=conv_beta, w_out=ev_out, ln_g=lng, ln_b=lnb, layer=i, j=j)
            h = mix(h, mod, lat_row, yl, ql, kl, vl, latent=True, tq=tq_lat)
            if not last:
                hc = mix(hc, mod, ctx_row, yc, qc, None, None, latent=False, tq=CTX_LEN)
        else:
            decay = ret_decay[j]
            qc, kc, vc, sgc = _ret_proj(hc, mod, ctx_row, ret_in, i, j, None, tm_ctx)
            ql, kl, vl, sgl = _ret_proj(h, mod, lat_row, ret_in, i, j, rope_r, tm_lat)
            hc_next, states = _ret_mix(decay, qc, kc, vc, sgc, hc, mod, ctx_row, None, ret_out, lng, lnb, i, j, True)
            h, = _ret_mix(decay, ql, kl, vl, sgl, h, mod, lat_row, states, ret_out, lng, lnb, i, j, False)
            if not last:
                hc = hc_next
        h = ffn(h, lat_row, tm_lat, i, 1)
        if not last:
            hc = ffn(hc, ctx_row, tm_ctx, i, 1)
    return h
```

```python
import functools

import jax
import jax.numpy as jnp
from jax import lax
from jax.experimental import pallas as pl
from jax.experimental.pallas import tpu as pltpu

F32 = jnp.float32
BF16 = jnp.bfloat16

D_MODEL = 1024
DEPTH = 4
GRID_W = 64
CTX_LEN = 256
CONV_DIM = 512
CONV_WIDTH = 31
CONV_HALO = 16
ATT_HEADS = 8
ATT_KV_HEADS = 2
HEAD_DIM = 64
ATT_DIM = ATT_HEADS * HEAD_DIM
KV_DIM = ATT_KV_HEADS * HEAD_DIM
WINDOW = 128
BLOCK = 128
ATT_SCALE = HEAD_DIM ** -0.5
ROPE_BASE = 10000.0
KV_OFF = 2 * CONV_DIM + ATT_DIM
MIX_EVEN = CONV_DIM + ATT_DIM
RET_HEADS = 4
RET_DK = 256
RET_DV = 512
RET_QK = RET_HEADS * RET_DK
RET_V = RET_HEADS * RET_DV
D_FF = 2816
DEEPNORM_ALPHA = (2 * DEPTH) ** 0.25
LN_EPS = 1e-5
NEG_INF = -1e30

LANES = 128
SUBLANES = 8
MOD_ROWS = 16
VMEM_LIMIT_BYTES = 56 * 1024 * 1024
FFN_CHUNK = 256


def _params(*sem):
    return pltpu.CompilerParams(dimension_semantics=sem, vmem_limit_bytes=VMEM_LIMIT_BYTES)


def _dot(a, b):
    return jnp.dot(a, b, preferred_element_type=F32)


def _dot_nt(a, b):
    return lax.dot_general(a, b, (((1,), (1,)), ((), ())), preferred_element_type=F32)


def _dot_tn(a, b):
    return lax.dot_general(a, b, (((0,), (0,)), ((), ())), preferred_element_type=F32)


def _silu(x):
    return x * jax.nn.sigmoid(x)


def _standardize(x):
    mu = jnp.mean(x, axis=-1, keepdims=True)
    d = x - mu
    var = jnp.mean(d * d, axis=-1, keepdims=True)
    return d * lax.rsqrt(var + LN_EPS)


def _post_norm(h, out, g, b):
    return _standardize(DEEPNORM_ALPHA * h + out) * g + b


def _resident(shape, index_map):
    return pl.BlockSpec(shape, index_map, pipeline_mode=pl.Buffered(1))


def _mod_kernel(c_ref, w_ref, b_ref, o_ref):
    s = _silu(c_ref[...]).astype(BF16)
    o_ref[0] = _dot(s, w_ref[0].astype(BF16)) + b_ref[0]


def _modulation(cvec, w_ada, b_ada):
    tn = 1536
    n_out = 9 * D_MODEL
    out = pl.pallas_call(
        _mod_kernel,
        grid=(DEPTH, n_out // tn),
        in_specs=[
            pl.BlockSpec((MOD_ROWS, D_MODEL), lambda i, n: (0, 0)),
            pl.BlockSpec((1, D_MODEL, tn), lambda i, n: (i, 0, n)),
            pl.BlockSpec((1, 1, tn), lambda i, n: (i, 0, n)),
        ],
        out_specs=pl.BlockSpec((1, MOD_ROWS, tn), lambda i, n: (i, 0, n)),
        out_shape=jax.ShapeDtypeStruct((DEPTH, MOD_ROWS, n_out), F32),
        compiler_params=_params("parallel", "parallel"),
        name="adaln_modulation",
    )(cvec, w_ada, b_ada.reshape(DEPTH, 1, n_out))
    return out.reshape(DEPTH, MOD_ROWS, 9, D_MODEL)


def _mod_spec(layer, row_of_batch):
    return pl.BlockSpec((None, None, 9, D_MODEL), lambda b, t: (layer, row_of_batch(b), 0, 0))


def _ln_spec(layer, j):
    return pl.BlockSpec((None, 1, D_MODEL), lambda b, t: (layer * 3 + j, 0, 0))


def _ffn_kernel(h_ref, mod_ref, win_ref, wout_ref, g_ref, b_ref, o_ref, act_ref, *, k0):
    h = h_ref[...]
    shift, scale, gate = mod_ref[k0:k0 + 1], mod_ref[k0 + 1:k0 + 2], mod_ref[k0 + 2:k0 + 3]
    u = (h * (1.0 + scale) + shift).astype(BF16)
    for n in range(D_FF // FFN_CHUNK):
        lo = n * FFN_CHUNK
        gt = _dot(u, win_ref[:, lo:lo + FFN_CHUNK])
        up = _dot(u, win_ref[:, D_FF + lo:D_FF + lo + FFN_CHUNK])
        act_ref[:, lo:lo + FFN_CHUNK] = (_silu(gt) * up).astype(BF16)
    y = _dot(act_ref[...], wout_ref[...])
    o_ref[...] = _post_norm(h, (0.5 * gate) * y, g_ref[...], b_ref[...])


def _ffn_half(h, mod, row_of_batch, w_in, w_out, ln_g, ln_b, layer, j, k0, tm):
    bsz, n, _ = h.shape
    return pl.pallas_call(
        functools.partial(_ffn_kernel, k0=k0),
        grid=(bsz, n // tm),
        in_specs=[
            pl.BlockSpec((None, tm, D_MODEL), lambda b, t: (b, t, 0)),
            _mod_spec(layer, row_of_batch),
            _resident((None, None, D_MODEL, 2 * D_FF), lambda b, t: (layer, j, 0, 0)),
            _resident((None, None, D_FF, D_MODEL), lambda b, t: (layer, j, 0, 0)),
            _ln_spec(layer, 2 * j),
            _ln_spec(layer, 2 * j),
        ],
        out_specs=pl.BlockSpec((None, tm, D_MODEL), lambda b, t: (b, t, 0)),
        out_shape=jax.ShapeDtypeStruct(h.shape, F32),
        scratch_shapes=[pltpu.VMEM((tm, D_FF), BF16)],
        compiler_params=_params("parallel", "parallel"),
        name="ffn_half",
    )(h, mod, w_in, w_out, ln_g, ln_b)


EVEN_COLS = 2 * CONV_DIM + ATT_DIM + 4 * KV_DIM


def _rope64(x, cos, sin):
    width = x.shape[-1]
    reps = width // LANES
    cosw = jnp.concatenate([cos] * reps, axis=1) if reps > 1 else cos
    sinw = jnp.concatenate([sin] * reps, axis=1) if reps > 1 else sin
    lane = lax.broadcasted_iota(jnp.int32, x.shape, 1)
    first_half = (lane % HEAD_DIM) < (HEAD_DIM // 2)
    partner = jnp.where(first_half, pltpu.roll(x, width - HEAD_DIM // 2, 1), pltpu.roll(x, HEAD_DIM // 2, 1))
    return x * cosw + partner * sinw


CONV_ROWS = 64
CONV_SHIFT_PAD = (CONV_HALO - CONV_WIDTH // 2 + CONV_WIDTH - 1) // SUBLANES * SUBLANES
CONV_GROUP = 256
LOG2E = 1.4426950408889634


def _conv_taps(ybuf_ref, ysh_ref, conv_ref, cw_ref, cb_ref, cols, tm):
    for r in range(SUBLANES):
        ysh_ref[r, :, cols] = ybuf_ref[r:r + tm + CONV_SHIFT_PAD, cols]
    first = CONV_HALO - CONV_WIDTH // 2
    for c in range(cols.start // LANES, cols.stop // LANES):
        cl = slice(c * LANES, (c + 1) * LANES)
        for r0 in range(0, tm, CONV_ROWS):
            acc = jnp.zeros((CONV_ROWS, LANES), F32) + cb_ref[:, cl]
            for j in range(CONV_WIDTH):
                a, r = divmod(first + j, SUBLANES)
                lo = r0 + a * SUBLANES
                acc = acc + ysh_ref[r, lo:lo + CONV_ROWS, cl] * cw_ref[j:j + 1, cl]
            conv_ref[r0:r0 + CONV_ROWS, cl] = acc


def _even_proj_kernel(*refs, rope, tm, n_tiles):
    refs = list(refs)
    hprev_ref, h_ref, hnext_ref, mod_ref, w_ref = refs[:5]
    pos = 5
    if rope:
        cos_ref, sin_ref = refs[pos:pos + 2]
        pos += 2
    (cw_ref, cb_ref, cg_ref, cbeta_ref, c_ref, q_ref, k_ref, v_ref, ybuf_ref, ysh_ref, conv_ref) = refs[pos:]
    t = pl.program_id(1)

    def modulated(x):
        return (x * (1.0 + mod_ref[4:5]) + mod_ref[3:4]).astype(BF16)

    u = modulated(h_ref[...])
    u_ext = jnp.concatenate([modulated(hprev_ref[...]), u, modulated(hnext_ref[...])], axis=0)
    erow = lax.broadcasted_iota(jnp.int32, (tm + 2 * CONV_HALO, 1), 0)
    inside = (erow >= jnp.where(t > 0, 0, CONV_HALO)) & (erow < jnp.where(t < n_tiles - 1, tm + 2 * CONV_HALO, tm + CONV_HALO))
    for g0 in range(0, CONV_DIM, CONV_GROUP):
        cols = slice(g0, g0 + CONV_GROUP)
        a = _dot(u_ext, w_ref[:, g0:g0 + CONV_GROUP])
        gt = _dot(u_ext, w_ref[:, CONV_DIM + g0:CONV_DIM + g0 + CONV_GROUP])
        ybuf_ref[:, cols] = jnp.where(inside, a * jax.nn.sigmoid(gt), 0.0)
        _conv_taps(ybuf_ref, ysh_ref, conv_ref, cw_ref, cb_ref, cols, tm)
    c_ref[...] = _silu(_standardize(conv_ref[...]) * cg_ref[...] + cbeta_ref[...]).astype(BF16)
    q = _dot(u, w_ref[:, 2 * CONV_DIM:KV_OFF])
    k = _dot(u, w_ref[:, KV_OFF:KV_OFF + 2 * KV_DIM])
    v = _dot(u, w_ref[:, KV_OFF + 2 * KV_DIM:EVEN_COLS])
    if rope:
        cos, sin = cos_ref[...], sin_ref[...]
        q = _rope64(q, cos, sin)
        k = _rope64(k, cos, sin)
    q_ref[...] = (q * (ATT_SCALE * LOG2E)).astype(BF16)
    k_ref[...] = k.astype(BF16)
    v_ref[...] = v.astype(BF16)


def _even_proj(h, mod, row_of_batch, w, cw, cb, cg, cbeta, layer, j, rope_tables, tm):
    bsz, n, _ = h.shape
    n_tiles = n // tm
    hb = tm // CONV_HALO
    rope = rope_tables is not None
    small = lambda rows: pl.BlockSpec((None, rows, CONV_DIM), lambda b, t: (j, 0, 0))
    in_specs = [
        pl.BlockSpec((None, CONV_HALO, D_MODEL), lambda b, t: (b, jnp.maximum(t * hb - 1, 0), 0)),
        pl.BlockSpec((None, tm, D_MODEL), lambda b, t: (b, t, 0)),
        pl.BlockSpec((None, CONV_HALO, D_MODEL), lambda b, t: (b, jnp.minimum((t + 1) * hb, n // CONV_HALO - 1), 0)),
        _mod_spec(layer, row_of_batch),
        _resident((None, D_MODEL, EVEN_COLS), lambda b, t: (j, 0, 0)),
    ]
    args = [h, h, h, mod, w]
    if rope:
        in_specs += [pl.BlockSpec((tm, LANES), lambda b, t: (t, 0))] * 2
        args += list(rope_tables)
    in_specs += [small(CONV_WIDTH), small(1), small(1), small(1)]
    args += [cw, cb, cg, cbeta]

    def out(width):
        return (pl.BlockSpec((None, tm, width), lambda b, t: (b, t, 0)),
                jax.ShapeDtypeStruct((bsz, n, width), BF16))

    outs = [out(CONV_DIM), out(ATT_DIM), out(2 * KV_DIM), out(2 * KV_DIM)]
    return pl.pallas_call(
        functools.partial(_even_proj_kernel, rope=rope, tm=tm, n_tiles=n_tiles),
        grid=(bsz, n_tiles),
        in_specs=in_specs,
        out_specs=[o[0] for o in outs],
        out_shape=[o[1] for o in outs],
        scratch_shapes=[pltpu.VMEM((tm + 2 * CONV_HALO, CONV_DIM), F32),
                        pltpu.VMEM((SUBLANES, tm + CONV_SHIFT_PAD, CONV_DIM), F32),
                        pltpu.VMEM((tm, CONV_DIM), F32)],
        compiler_params=_params("parallel", "parallel"),
        name="even_proj",
    )(*args)


def _attend(qe, kloc, vloc, kctx, vctx, sink_col, valid):
    s_ctx = _dot_nt(qe, kctx)
    m = jnp.maximum(jnp.max(s_ctx, axis=-1, keepdims=True), sink_col)
    if kloc is not None:
        s_loc = jnp.where(valid, _dot_nt(qe, kloc), NEG_INF)
        m = jnp.maximum(m, jnp.max(s_loc, axis=-1, keepdims=True))
        p_loc = jnp.exp2(s_loc - m)
    p_ctx = jnp.exp2(s_ctx - m)
    den = jnp.sum(p_ctx, axis=-1, keepdims=True) + jnp.exp2(sink_col - m)
    o = _dot(p_ctx.astype(BF16), vctx)
    if kloc is not None:
        den = den + jnp.sum(p_loc, axis=-1, keepdims=True)
        o = o + _dot(p_loc.astype(BF16), vloc)
    return o / den


def _even_mix_kernel(*refs, latent, tq, n_tiles):
    if latent:
        (sink_ref, h_ref, mod_ref, c_ref, q_ref,
         kprev_ref, kcur_ref, knext_ref, vprev_ref, vcur_ref, vnext_ref, kctx_ref, vctx_ref,
         wout_ref, g_ref, b_ref, o_ref, kk_ref, vv_ref, cat_ref) = refs
    else:
        (sink_ref, h_ref, mod_ref, c_ref, q_ref, kctx_ref, vctx_ref,
         wout_ref, g_ref, b_ref, o_ref, cat_ref) = refs
    i = pl.program_id(1)
    if latent:
        kk_ref[0:BLOCK] = kprev_ref[...]
        kk_ref[BLOCK:BLOCK + tq] = kcur_ref[...]
        kk_ref[BLOCK + tq:] = knext_ref[...]
        vv_ref[0:BLOCK] = vprev_ref[...]
        vv_ref[BLOCK:BLOCK + tq] = vcur_ref[...]
        vv_ref[BLOCK + tq:] = vnext_ref[...]
    cat_ref[:, 0:CONV_DIM] = c_ref[...]

    rows = 2 * BLOCK
    lane = lax.broadcasted_iota(jnp.int32, (rows, LANES), 1)
    row = lax.broadcasted_iota(jnp.int32, (rows, 1), 0)
    qrow = lax.broadcasted_iota(jnp.int32, (rows, 3 * BLOCK), 0) % BLOCK
    kcol = lax.broadcasted_iota(jnp.int32, (rows, 3 * BLOCK), 1)
    band = jnp.abs(kcol - BLOCK - qrow) <= WINDOW
    for n in range(tq // BLOCK):
        valid = None
        if latent:
            lo = 0 if n > 0 else jnp.where(i > 0, 0, BLOCK)
            hi = 3 * BLOCK if n < tq // BLOCK - 1 else jnp.where(i < n_tiles - 1, 3 * BLOCK, 2 * BLOCK)
            valid = band & (kcol >= lo) & (kcol < hi)
        for hk in range(ATT_KV_HEADS):
            kvl = slice(hk * LANES, (hk + 1) * LANES)
            q2 = jnp.concatenate(
                [q_ref[n * BLOCK:(n + 1) * BLOCK, (2 * hk + p) * LANES:(2 * hk + p + 1) * LANES] for p in range(2)],
                axis=0)
            kloc = vloc = None
            if latent:
                kloc = kk_ref[n * BLOCK:(n + 3) * BLOCK, kvl]
                vloc = vv_ref[n * BLOCK:(n + 3) * BLOCK, kvl]
            outs = []
            for e in range(2):
                in_e = (lane >= HEAD_DIM) if e else (lane < HEAD_DIM)
                qe = jnp.where(in_e, q2, jnp.zeros_like(q2))
                head0 = hk * 4 + e
                sink_col = jnp.where(row < BLOCK, sink_ref[head0] * LOG2E, sink_ref[head0 + 2] * LOG2E)
                outs.append(_attend(qe, kloc, vloc, kctx_ref[:, kvl], vctx_ref[:, kvl], sink_col, valid))
            o2 = jnp.where(lane < HEAD_DIM, outs[0], outs[1]).astype(BF16)
            for p in range(2):
                c0 = CONV_DIM + (2 * hk + p) * LANES
                cat_ref[n * BLOCK:(n + 1) * BLOCK, c0:c0 + LANES] = o2[p * BLOCK:(p + 1) * BLOCK]
    y = _dot(cat_ref[...], wout_ref[...])
    o_ref[...] = _post_norm(h_ref[...], mod_ref[5:6] * y, g_ref[...], b_ref[...])


def _even_mix(h, mod, row_of_batch, conv, q, k, v, kctx, vctx, sink, w_out, ln_g, ln_b, layer, j, latent, tq):
    bsz, n, _ = h.shape
    n_tiles = n // tq
    kb = tq // BLOCK

    def tile(width):
        return pl.BlockSpec((None, tq, width), lambda b, t: (b, t, 0))

    def prev(rows, per_tile, width):
        return pl.BlockSpec((None, rows, width), lambda b, t: (b, jnp.maximum(t * per_tile - 1, 0), 0))

    def nxt(rows, per_tile, width):
        last = n // rows - 1
        return pl.BlockSpec((None, rows, width), lambda b, t: (b, jnp.minimum((t + 1) * per_tile, last), 0))

    ctx_kv = pl.BlockSpec((None, CTX_LEN, 2 * KV_DIM), lambda b, t: (b, 0, 0))
    in_specs = [pl.BlockSpec(memory_space=pltpu.SMEM), tile(D_MODEL), _mod_spec(layer, row_of_batch),
                tile(CONV_DIM), tile(ATT_DIM)]
    args = [sink, h, mod, conv, q]
    if latent:
        in_specs += [prev(BLOCK, kb, 2 * KV_DIM), tile(2 * KV_DIM), nxt(BLOCK, kb, 2 * KV_DIM),
                     prev(BLOCK, kb, 2 * KV_DIM), tile(2 * KV_DIM), nxt(BLOCK, kb, 2 * KV_DIM)]
        args += [k, k, k, v, v, v]
    in_specs += [ctx_kv, ctx_kv, _resident((None, MIX_EVEN, D_MODEL), lambda b, t: (j, 0, 0)),
                 _ln_spec(layer, 1), _ln_spec(layer, 1)]
    args += [kctx, vctx, w_out, ln_g, ln_b]
    scratch = [pltpu.VMEM((tq + 2 * BLOCK, 2 * KV_DIM), BF16)] * 2 if latent else []
    scratch += [pltpu.VMEM((tq, MIX_EVEN), BF16)]
    return pl.pallas_call(
        functools.partial(_even_mix_kernel, latent=latent, tq=tq, n_tiles=n_tiles),
        grid=(bsz, n_tiles),
        in_specs=in_specs,
        out_specs=tile(D_MODEL),
        out_shape=jax.ShapeDtypeStruct(h.shape, F32),
        scratch_shapes=scratch,
        compiler_params=_params("parallel", "parallel"),
        name="even_mix",
    )(*args)


RET_COLS = 2 * RET_QK + 2 * RET_V


def _rope256(x, cos, sin):
    half = RET_DK // 2
    parts = []
    for hd in range(RET_HEADS):
        x1 = x[:, hd * RET_DK:hd * RET_DK + half]
        x2 = x[:, hd * RET_DK + half:(hd + 1) * RET_DK]
        parts += [x1 * cos - x2 * sin, x2 * cos + x1 * sin]
    return jnp.concatenate(parts, axis=1)


def _ret_proj_kernel(*refs, rope):
    if rope:
        h_ref, mod_ref, w_ref, cos_ref, sin_ref, q_ref, k_ref, v_ref, sg_ref = refs
    else:
        h_ref, mod_ref, w_ref, q_ref, k_ref, v_ref, sg_ref = refs
    u = (h_ref[...] * (1.0 + mod_ref[4:5]) + mod_ref[3:4]).astype(BF16)
    q = _dot(u, w_ref[:, 0:RET_QK])
    k = _dot(u, w_ref[:, RET_QK:2 * RET_QK])
    if rope:
        cos, sin = cos_ref[...], sin_ref[...]
        q = _rope256(q, cos, sin)
        k = _rope256(k, cos, sin)
    q_ref[...] = (q * (RET_DK ** -0.5)).astype(BF16)
    k_ref[...] = k.astype(BF16)
    v_ref[...] = _dot(u, w_ref[:, 2 * RET_QK:2 * RET_QK + RET_V]).astype(BF16)
    sg_ref[...] = _silu(_dot(u, w_ref[:, 2 * RET_QK + RET_V:RET_COLS])).astype(BF16)


def _ret_proj(h, mod, row_of_batch, w, layer, j, rope_tables, tm):
    bsz, n, _ = h.shape
    rope = rope_tables is not None
    in_specs = [
        pl.BlockSpec((None, tm, D_MODEL), lambda b, t: (b, t, 0)),
        _mod_spec(layer, row_of_batch),
        _resident((None, D_MODEL, RET_COLS), lambda b, t: (j, 0, 0)),
    ]
    args = [h, mod, w]
    if rope:
        in_specs += [pl.BlockSpec((tm, LANES), lambda b, t: (t, 0))] * 2
        args += list(rope_tables)
    widths = [RET_QK, RET_QK, RET_V, RET_V]
    return pl.pallas_call(
        functools.partial(_ret_proj_kernel, rope=rope),
        grid=(bsz, n // tm),
        in_specs=in_specs,
        out_specs=[pl.BlockSpec((None, tm, wd), lambda b, t: (b, t, 0)) for wd in widths],
        out_shape=[jax.ShapeDtypeStruct((bsz, n, wd), BF16) for wd in widths],
        compiler_params=_params("parallel", "parallel"),
        name="ret_proj",
    )(*args)


RET_BLOCK = 256


def _ret_mix_kernel(*refs, nc, has_init, out_state):
    refs = list(refs)
    decay_ref, q_ref, k_ref, v_ref, sg_ref, h_ref, mod_ref = refs[:7]
    pos = 7
    init_ref = fin_ref = None
    if has_init:
        init_ref = refs[pos]
        pos += 1
    w_ref, g_ref, b_ref, o_ref = refs[pos:pos + 4]
    pos += 4
    if out_state:
        fin_ref = refs[pos]
        pos += 1
    state_ref, dmat_ref, qdec_ref, kdec_ref, cdec_ref, ob_ref, z_ref = refs[pos:]
    s = pl.program_id(1)
    rb = RET_BLOCK

    def load_state(d):
        for hd in range(RET_HEADS):
            state_ref[hd] = init_ref[d, hd] if has_init else jnp.zeros((RET_DK, RET_DV), F32)

    @pl.when(s == 0)
    def _():
        prow = lax.broadcasted_iota(jnp.int32, (rb, rb), 0)
        pcol = lax.broadcasted_iota(jnp.int32, (rb, rb), 1)
        posf = lax.broadcasted_iota(jnp.int32, (rb, LANES), 0).astype(F32)
        for d in range(2):
            rel = (pcol - prow) if d else (prow - pcol)
            keep = (rel > 0) if d else (rel >= 0)
            relf = jnp.where(keep, rel, 0).astype(F32)
            qpow = (rb - posf) if d else (posf + 1.0)
            kpow = posf if d else (rb - 1.0 - posf)
            for hd in range(RET_HEADS):
                def log_gamma(shape):
                    return -jnp.log(1.0 + jnp.exp(-jnp.full(shape, decay_ref[d, hd], F32)))

                dmat_ref[d, hd] = jnp.where(keep, jnp.exp(log_gamma((rb, rb)) * relf), 0.0)
                qdec_ref[d, hd] = jnp.exp(log_gamma((rb, LANES)) * qpow)
                kdec_ref[d, hd] = jnp.exp(log_gamma((rb, LANES)) * kpow)
                cdec_ref[d, hd] = jnp.exp(log_gamma((8, LANES)) * float(rb))
        load_state(1)

    @pl.when(s == nc)
    def _():
        if out_state:
            fin_ref[1] = state_ref[...]
        load_state(0)

    def lanes(t, width):
        return jnp.concatenate([t] * (width // LANES), axis=1)

    def head_out(d, hd):
        qh = q_ref[:, hd * RET_DK:(hd + 1) * RET_DK]
        kh = k_ref[:, hd * RET_DK:(hd + 1) * RET_DK]
        vh = v_ref[:, hd * RET_DV:(hd + 1) * RET_DV]
        st = state_ref[hd]
        sc = _dot_nt(qh, kh) * dmat_ref[d, hd]
        o = _dot(sc.astype(BF16), vh) + _dot((qh * lanes(qdec_ref[d, hd], RET_DK)).astype(BF16), st.astype(BF16))
        kd = (kh * lanes(kdec_ref[d, hd], RET_DK)).astype(BF16)
        state_ref[hd] = st * lanes(cdec_ref[d, hd, 0:1, :], RET_DV) + _dot_tn(kd, vh)
        return o

    @pl.when(s < nc)
    def _():
        row0 = pl.multiple_of((nc - 1 - s) * rb, rb)
        for hd in range(RET_HEADS):
            ob_ref[pl.ds(row0, rb), hd * RET_DV:(hd + 1) * RET_DV] = head_out(1, hd).astype(BF16)

    @pl.when(s >= nc)
    def _():
        row0 = pl.multiple_of((s - nc) * rb, rb)
        for hd in range(RET_HEADS):
            cols = slice(hd * RET_DV, (hd + 1) * RET_DV)
            on = _standardize(head_out(0, hd) + ob_ref[pl.ds(row0, rb), cols].astype(F32))
            z_ref[:, cols] = (sg_ref[:, cols] * on).astype(BF16)
        y = _dot(z_ref[...], w_ref[...])
        o_ref[...] = _post_norm(h_ref[...], mod_ref[5:6] * y, g_ref[...], b_ref[...])

    if out_state:
        @pl.when(s == 2 * nc - 1)
        def _():
            fin_ref[0] = state_ref[...]


def _ret_mix(decay, q, k, v, sg, h, mod, row_of_batch, init, w, ln_g, ln_b, layer, j, out_state):
    bsz, n, _ = q.shape
    rb = RET_BLOCK
    nc = n // rb
    chunk = lambda b, s: (b, jnp.where(s < nc, nc - 1 - s, s - nc), 0)
    fwd = lambda b, s: (b, jnp.maximum(s - nc, 0), 0)
    state_shape = (None, 2, RET_HEADS, RET_DK, RET_DV)
    state_index = lambda b, s: (b, 0, 0, 0, 0)
    in_specs = [pl.BlockSpec(memory_space=pltpu.SMEM),
                pl.BlockSpec((None, rb, RET_QK), chunk),
                pl.BlockSpec((None, rb, RET_QK), chunk),
                pl.BlockSpec((None, rb, RET_V), chunk),
                pl.BlockSpec((None, rb, RET_V), fwd),
                pl.BlockSpec((None, rb, D_MODEL), fwd),
                _mod_spec(layer, row_of_batch)]
    args = [decay, q, k, v, sg, h, mod]
    if init is not None:
        in_specs.append(_resident(state_shape, state_index))
        args.append(init)
    in_specs += [_resident((None, RET_V, D_MODEL), lambda b, s: (j, 0, 0)), _ln_spec(layer, 1), _ln_spec(layer, 1)]
    args += [w, ln_g, ln_b]
    out_specs = [pl.BlockSpec((None, rb, D_MODEL), fwd)]
    out_shape = [jax.ShapeDtypeStruct(h.shape, F32)]
    if out_state:
        out_specs.append(pl.BlockSpec(state_shape, state_index))
        out_shape.append(jax.ShapeDtypeStruct((bsz, 2, RET_HEADS, RET_DK, RET_DV), F32))
    return pl.pallas_call(
        functools.partial(_ret_mix_kernel, nc=nc, has_init=init is not None, out_state=out_state),
        grid=(bsz, 2 * nc),
        in_specs=in_specs,
        out_specs=out_specs,
        out_shape=out_shape,
        scratch_shapes=[pltpu.VMEM((RET_HEADS, RET_DK, RET_DV), F32),
                        pltpu.VMEM((2, RET_HEADS, rb, rb), F32),
                        pltpu.VMEM((2, RET_HEADS, rb, LANES), F32),
                        pltpu.VMEM((2, RET_HEADS, rb, LANES), F32),
                        pltpu.VMEM((2, RET_HEADS, 8, LANES), F32),
                        pltpu.VMEM((n, RET_V), BF16),
                        pltpu.VMEM((rb, RET_V), BF16)],
        compiler_params=_params("parallel", "arbitrary"),
        name="ret_mix",
    )(*args)


def _axial_rope_tables(n_tokens):
    rows = n_tokens // GRID_W
    r = jnp.broadcast_to(jnp.arange(rows, dtype=F32)[:, None], (rows, GRID_W)).reshape(-1)
    cidx = jnp.broadcast_to(jnp.arange(GRID_W, dtype=F32)[None, :], (rows, GRID_W)).reshape(-1)
    nf = HEAD_DIM // 4
    inv = ROPE_BASE ** (-jnp.arange(nf, dtype=F32) / nf)
    ang = jnp.concatenate([r[:, None] * inv[None], cidx[:, None] * inv[None]], axis=-1)
    cos, sin = jnp.cos(ang), jnp.sin(ang)
    reps = LANES // HEAD_DIM
    return jnp.tile(jnp.concatenate([cos, cos], axis=-1), (1, reps)), jnp.tile(jnp.concatenate([-sin, sin], axis=-1), (1, reps))


def _retention_rope_tables(n_tokens):
    inv = 1.0 / (ROPE_BASE ** jnp.linspace(0.0, 1.0, RET_DK // 2, dtype=F32))
    ang = jnp.arange(n_tokens, dtype=F32)[:, None] * inv[None]
    return jnp.cos(ang), jnp.sin(ang)


def _even_weight_layout(ev_w_in):
    def dup(cols):
        heads = [cols[..., hd * HEAD_DIM:(hd + 1) * HEAD_DIM] for hd in range(ATT_KV_HEADS)]
        return jnp.concatenate([t for hd in heads for t in (hd, hd)], axis=-1)

    return jnp.concatenate([ev_w_in[..., :KV_OFF], dup(ev_w_in[..., KV_OFF:KV_OFF + KV_DIM]),
                            dup(ev_w_in[..., KV_OFF + KV_DIM:])], axis=-1).astype(BF16)


def kernel(x, c, ctx, c_ctx, w_ada, b_ada, ln_g, ln_b, ffn_w_in, ffn_w_out, ev_w_in, ev_conv_w, ev_conv_b,
           ev_norm_g, ev_norm_b, ev_sink, ev_w_out, ret_w_in, ret_decay, ret_w_out):
    bsz, n_tok, _ = x.shape
    assert bsz < MOD_ROWS and ctx.shape[1] == CTX_LEN
    rope_a = _axial_rope_tables(n_tok)
    rope_r = _retention_rope_tables(n_tok)
    cvec = jnp.zeros((MOD_ROWS, D_MODEL), F32).at[:bsz].set(c).at[bsz].set(c_ctx)
    mod = _modulation(cvec, w_ada, b_ada)
    lat_row = lambda b: b
    ctx_row = lambda b: bsz

    ffn_in = ffn_w_in.astype(BF16)
    ffn_out = ffn_w_out.astype(BF16)
    ev_in = _even_weight_layout(ev_w_in)
    ev_out = ev_w_out.astype(BF16)
    ret_in = ret_w_in.astype(BF16)
    ret_out = ret_w_out.astype(BF16)
    lng = ln_g.reshape(DEPTH * 3, 1, D_MODEL)
    lnb = ln_b.reshape(DEPTH * 3, 1, D_MODEL)
    conv_b = ev_conv_b[:, None, :]
    conv_g = ev_norm_g[:, None, :]
    conv_beta = ev_norm_b[:, None, :]

    tm_lat = min(512, n_tok)
    tm_ctx = CTX_LEN
    tq_lat = min(512, n_tok)

    def ffn(hh, row, tm, layer, j):
        return _ffn_half(hh, mod, row, ffn_in, ffn_out, lng, lnb, layer, j, 6 * j, tm)

    h, hc = x, ctx
    for i in range(DEPTH):
        last = i == DEPTH - 1
        j = i // 2
        h = ffn(h, lat_row, tm_lat, i, 0)
        hc = ffn(hc, ctx_row, tm_ctx, i, 0)
        if i % 2 == 0:
            proj = functools.partial(_even_proj, w=ev_in, cw=ev_conv_w, cb=conv_b, cg=conv_g, cbeta=conv_beta,
                                     layer=i, j=j)
            yc, qc, kc, vc = proj(hc, mod, ctx_row, rope_tables=None, tm=tm_ctx)
            yl, ql, kl, vl = proj(h, mod, lat_row, rope_tables=rope_a, tm=tm_lat)
            mix = functools.partial(_even_mix, kctx=kc, vctx=vc, sink=ev_sink[j], w_out=ev_out, ln_g=lng, ln_b=lnb,
                                    layer=i, j=j)
            h = mix(h, mod, lat_row, yl, ql, kl, vl, latent=True, tq=tq_lat)
            if not last:
                hc = mix(hc, mod, ctx_row, yc, qc, None, None, latent=False, tq=CTX_LEN)
        else:
            decay = ret_decay[j]
            qc, kc, vc, sgc = _ret_proj(hc, mod, ctx_row, ret_in, i, j, None, tm_ctx)
            ql, kl, vl, sgl = _ret_proj(h, mod, lat_row, ret_in, i, j, rope_r, tm_lat)
            hc_next, states = _ret_mix(decay, qc, kc, vc, sgc, hc, mod, ctx_row, None, ret_out, lng, lnb, i, j, True)
            h, = _ret_mix(decay, ql, kl, vl, sgl, h, mod, lat_row, states, ret_out, lng, lnb, i, j, False)
            if not last:
                hc = hc_next
        h = ffn(h, lat_row, tm_lat, i, 1)
        if not last:
            hc = ffn(hc, ctx_row, tm_ctx, i, 1)
    return h
```

```python
import functools

import jax
import jax.numpy as jnp
from jax import lax
from jax.experimental import pallas as pl
from jax.experimental.pallas import tpu as pltpu

F32 = jnp.float32
BF16 = jnp.bfloat16

D_MODEL = 1024
DEPTH = 4
GRID_W = 64
CTX_LEN = 256
CONV_DIM = 512
CONV_WIDTH = 31
CONV_HALO = 16
ATT_HEADS = 8
ATT_KV_HEADS = 2
HEAD_DIM = 64
ATT_DIM = ATT_HEADS * HEAD_DIM
KV_DIM = ATT_KV_HEADS * HEAD_DIM
WINDOW = 128
BLOCK = 128
ATT_SCALE = HEAD_DIM ** -0.5
ROPE_BASE = 10000.0
KV_OFF = 2 * CONV_DIM + ATT_DIM
MIX_EVEN = CONV_DIM + ATT_DIM
RET_HEADS = 4
RET_DK = 256
RET_DV = 512
RET_QK = RET_HEADS * RET_DK
RET_V = RET_HEADS * RET_DV
D_FF = 2816
DEEPNORM_ALPHA = (2 * DEPTH) ** 0.25
LN_EPS = 1e-5
NEG_INF = -1e30

LANES = 128
SUBLANES = 8
MOD_ROWS = 16
VMEM_LIMIT_BYTES = 56 * 1024 * 1024
FFN_CHUNK = 256


def _params(*sem):
    return pltpu.CompilerParams(dimension_semantics=sem, vmem_limit_bytes=VMEM_LIMIT_BYTES)


def _dot(a, b):
    return jnp.dot(a, b, preferred_element_type=F32)


def _dot_nt(a, b):
    return lax.dot_general(a, b, (((1,), (1,)), ((), ())), preferred_element_type=F32)


def _dot_tn(a, b):
    return lax.dot_general(a, b, (((0,), (0,)), ((), ())), preferred_element_type=F32)


def _silu(x):
    return x * jax.nn.sigmoid(x)


def _standardize(x):
    mu = jnp.mean(x, axis=-1, keepdims=True)
    d = x - mu
    var = jnp.mean(d * d, axis=-1, keepdims=True)
    return d * lax.rsqrt(var + LN_EPS)


def _post_norm(h, out, g, b):
    return _standardize(DEEPNORM_ALPHA * h + out) * g + b


def _resident(shape, index_map):
    return pl.BlockSpec(shape, index_map, pipeline_mode=pl.Buffered(1))


def _mod_kernel(c_ref, w_ref, b_ref, o_ref):
    s = _silu(c_ref[...]).astype(BF16)
    o_ref[0] = _dot(s, w_ref[0].astype(BF16)) + b_ref[0]


def _modulation(cvec, w_ada, b_ada):
    tn = 1536
    n_out = 9 * D_MODEL
    out = pl.pallas_call(
        _mod_kernel,
        grid=(DEPTH, n_out // tn),
        in_specs=[
            pl.BlockSpec((MOD_ROWS, D_MODEL), lambda i, n: (0, 0)),
            pl.BlockSpec((1, D_MODEL, tn), lambda i, n: (i, 0, n)),
            pl.BlockSpec((1, 1, tn), lambda i, n: (i, 0, n)),
        ],
        out_specs=pl.BlockSpec((1, MOD_ROWS, tn), lambda i, n: (i, 0, n)),
        out_shape=jax.ShapeDtypeStruct((DEPTH, MOD_ROWS, n_out), F32),
        compiler_params=_params("parallel", "parallel"),
        name="adaln_modulation",
    )(cvec, w_ada, b_ada.reshape(DEPTH, 1, n_out))
    return out.reshape(DEPTH, MOD_ROWS, 9, D_MODEL)


def _mod_spec(layer, row_of_batch):
    return pl.BlockSpec((None, None, 9, D_MODEL), lambda b, t: (layer, row_of_batch(b), 0, 0))


def _ln_spec(layer, j):
    return pl.BlockSpec((None, 1, D_MODEL), lambda b, t: (layer * 3 + j, 0, 0))


def _ffn_kernel(h_ref, mod_ref, win_ref, wout_ref, g_ref, b_ref, o_ref, act_ref, *, k0):
    h = h_ref[...]
    shift, scale, gate = mod_ref[k0:k0 + 1], mod_ref[k0 + 1:k0 + 2], mod_ref[k0 + 2:k0 + 3]
    u = (h * (1.0 + scale) + shift).astype(BF16)
    for n in range(D_FF // FFN_CHUNK):
        lo = n * FFN_CHUNK
        gt = _dot(u, win_ref[:, lo:lo + FFN_CHUNK])
        up = _dot(u, win_ref[:, D_FF + lo:D_FF + lo + FFN_CHUNK])
        act_ref[:, lo:lo + FFN_CHUNK] = (_silu(gt) * up).astype(BF16)
    y = _dot(act_ref[...], wout_ref[...])
    o_ref[...] = _post_norm(h, (0.5 * gate) * y, g_ref[...], b_ref[...])


def _ffn_half(h, mod, row_of_batch, w_in, w_out, ln_g, ln_b, layer, j, k0, tm):
    bsz, n, _ = h.shape
    return pl.pallas_call(
        functools.partial(_ffn_kernel, k0=k0),
        grid=(bsz, n // tm),
        in_specs=[
            pl.BlockSpec((None, tm, D_MODEL), lambda b, t: (b, t, 0)),
            _mod_spec(layer, row_of_batch),
            _resident((None, None, D_MODEL, 2 * D_FF), lambda b, t: (layer, j, 0, 0)),
            _resident((None, None, D_FF, D_MODEL), lambda b, t: (layer, j, 0, 0)),
            _ln_spec(layer, 2 * j),
            _ln_spec(layer, 2 * j),
        ],
        out_specs=pl.BlockSpec((None, tm, D_MODEL), lambda b, t: (b, t, 0)),
        out_shape=jax.ShapeDtypeStruct(h.shape, F32),
        scratch_shapes=[pltpu.VMEM((tm, D_FF), BF16)],
        compiler_params=_params("parallel", "parallel"),
        name="ffn_half",
    )(h, mod, w_in, w_out, ln_g, ln_b)


EVEN_COLS = 2 * CONV_DIM + ATT_DIM + 4 * KV_DIM


def _rope64(x, cos, sin):
    width = x.shape[-1]
    reps = width // LANES
    cosw = jnp.concatenate([cos] * reps, axis=1) if reps > 1 else cos
    sinw = jnp.concatenate([sin] * reps, axis=1) if reps > 1 else sin
    lane = lax.broadcasted_iota(jnp.int32, x.shape, 1)
    first_half = (lane % HEAD_DIM) < (HEAD_DIM // 2)
    partner = jnp.where(first_half, pltpu.roll(x, width - HEAD_DIM // 2, 1), pltpu.roll(x, HEAD_DIM // 2, 1))
    return x * cosw + partner * sinw


CONV_ROWS = 64
CONV_SHIFT_PAD = (CONV_HALO - CONV_WIDTH // 2 + CONV_WIDTH - 1) // SUBLANES * SUBLANES
CONV_GROUP = 256
LOG2E = 1.4426950408889634


def _conv_taps(ybuf_ref, ysh_ref, conv_ref, cw_ref, cb_ref, cols, tm):
    for r in range(1, SUBLANES):
        ysh_ref[r - 1, :, cols] = ybuf_ref[r:r + tm + CONV_SHIFT_PAD, cols]
    first = CONV_HALO - CONV_WIDTH // 2
    for c in range(cols.start // LANES, cols.stop // LANES):
        cl = slice(c * LANES, (c + 1) * LANES)
        for r0 in range(0, tm, CONV_ROWS):
            acc = jnp.zeros((CONV_ROWS, LANES), F32) + cb_ref[:, cl]
            for j in range(CONV_WIDTH):
                a, r = divmod(first + j, SUBLANES)
                lo = r0 + a * SUBLANES
                window = ysh_ref[r - 1, lo:lo + CONV_ROWS, cl] if r else ybuf_ref[lo:lo + CONV_ROWS, cl]
                acc = acc + window * cw_ref[j:j + 1, cl]
            conv_ref[r0:r0 + CONV_ROWS, cl] = acc


def _even_proj_kernel(*refs, rope, tm, n_tiles):
    refs = list(refs)
    hprev_ref, h_ref, hnext_ref, mod_ref, w_ref = refs[:5]
    pos = 5
    if rope:
        cos_ref, sin_ref = refs[pos:pos + 2]
        pos += 2
    (cw_ref, cb_ref, cg_ref, cbeta_ref, c_ref, q_ref, k_ref, v_ref, ybuf_ref, ysh_ref, conv_ref) = refs[pos:]
    t = pl.program_id(1)

    def modulated(x):
        return (x * (1.0 + mod_ref[4:5]) + mod_ref[3:4]).astype(BF16)

    u = modulated(h_ref[...])
    u_ext = jnp.concatenate([modulated(hprev_ref[...]), u, modulated(hnext_ref[...])], axis=0)
    erow = lax.broadcasted_iota(jnp.int32, (tm + 2 * CONV_HALO, 1), 0)
    inside = (erow >= jnp.where(t > 0, 0, CONV_HALO)) & (erow < jnp.where(t < n_tiles - 1, tm + 2 * CONV_HALO, tm + CONV_HALO))
    for g0 in range(0, CONV_DIM, CONV_GROUP):
        cols = slice(g0, g0 + CONV_GROUP)
        a = _dot(u_ext, w_ref[:, g0:g0 + CONV_GROUP])
        gt = _dot(u_ext, w_ref[:, CONV_DIM + g0:CONV_DIM + g0 + CONV_GROUP])
        ybuf_ref[:, cols] = jnp.where(inside, a * jax.nn.sigmoid(gt), 0.0)
        _conv_taps(ybuf_ref, ysh_ref, conv_ref, cw_ref, cb_ref, cols, tm)
    c_ref[...] = _silu(_standardize(conv_ref[...]) * cg_ref[...] + cbeta_ref[...]).astype(BF16)
    q = _dot(u, w_ref[:, 2 * CONV_DIM:KV_OFF])
    k = _dot(u, w_ref[:, KV_OFF:KV_OFF + 2 * KV_DIM])
    v = _dot(u, w_ref[:, KV_OFF + 2 * KV_DIM:EVEN_COLS])
    if rope:
        cos, sin = cos_ref[...], sin_ref[...]
        q = _rope64(q, cos, sin)
        k = _rope64(k, cos, sin)
    q_ref[...] = (q * (ATT_SCALE * LOG2E)).astype(BF16)
    k_ref[...] = k.astype(BF16)
    v_ref[...] = v.astype(BF16)


def _even_proj(h, mod, row_of_batch, w, cw, cb, cg, cbeta, layer, j, rope_tables, tm):
    bsz, n, _ = h.shape
    n_tiles = n // tm
    hb = tm // CONV_HALO
    rope = rope_tables is not None
    small = lambda rows: pl.BlockSpec((None, rows, CONV_DIM), lambda b, t: (j, 0, 0))
    in_specs = [
        pl.BlockSpec((None, CONV_HALO, D_MODEL), lambda b, t: (b, jnp.maximum(t * hb - 1, 0), 0)),
        pl.BlockSpec((None, tm, D_MODEL), lambda b, t: (b, t, 0)),
        pl.BlockSpec((None, CONV_HALO, D_MODEL), lambda b, t: (b, jnp.minimum((t + 1) * hb, n // CONV_HALO - 1), 0)),
        _mod_spec(layer, row_of_batch),
        _resident((None, D_MODEL, EVEN_COLS), lambda b, t: (j, 0, 0)),
    ]
    args = [h, h, h, mod, w]
    if rope:
        in_specs += [pl.BlockSpec((tm, LANES), lambda b, t: (t, 0))] * 2
        args += list(rope_tables)
    in_specs += [small(CONV_WIDTH), small(1), small(1), small(1)]
    args += [cw, cb, cg, cbeta]

    def out(width):
        return (pl.BlockSpec((None, tm, width), lambda b, t: (b, t, 0)),
                jax.ShapeDtypeStruct((bsz, n, width), BF16))

    outs = [out(CONV_DIM), out(ATT_DIM), out(2 * KV_DIM), out(2 * KV_DIM)]
    return pl.pallas_call(
        functools.partial(_even_proj_kernel, rope=rope, tm=tm, n_tiles=n_tiles),
        grid=(bsz, n_tiles),
        in_specs=in_specs,
        out_specs=[o[0] for o in outs],
        out_shape=[o[1] for o in outs],
        scratch_shapes=[pltpu.VMEM((tm + 2 * CONV_HALO, CONV_DIM), F32),
                        pltpu.VMEM((SUBLANES - 1, tm + CONV_SHIFT_PAD, CONV_DIM), F32),
                        pltpu.VMEM((tm, CONV_DIM), F32)],
        compiler_params=_params("parallel", "parallel"),
        name="even_proj",
    )(*args)


def _attend(qe, kloc, vloc, kctx, vctx, sink_col, valid):
    s_ctx = _dot_nt(qe, kctx)
    m = jnp.maximum(jnp.max(s_ctx, axis=-1, keepdims=True), sink_col)
    if kloc is not None:
        s_loc = jnp.where(valid, _dot_nt(qe, kloc), NEG_INF)
        m = jnp.maximum(m, jnp.max(s_loc, axis=-1, keepdims=True))
        p_loc = jnp.exp2(s_loc - m)
    p_ctx = jnp.exp2(s_ctx - m)
    den = jnp.sum(p_ctx, axis=-1, keepdims=True) + jnp.exp2(sink_col - m)
    o = _dot(p_ctx.astype(BF16), vctx)
    if kloc is not None:
        den = den + jnp.sum(p_loc, axis=-1, keepdims=True)
        o = o + _dot(p_loc.astype(BF16), vloc)
    return o / den


def _even_mix_kernel(*refs, latent, tq, n_tiles):
    if latent:
        (sink_ref, h_ref, mod_ref, c_ref, q_ref,
         kprev_ref, kcur_ref, knext_ref, vprev_ref, vcur_ref, vnext_ref, kctx_ref, vctx_ref,
         wout_ref, g_ref, b_ref, o_ref, kk_ref, vv_ref, cat_ref) = refs
    else:
        (sink_ref, h_ref, mod_ref, c_ref, q_ref, kctx_ref, vctx_ref,
         wout_ref, g_ref, b_ref, o_ref, cat_ref) = refs
    i = pl.program_id(1)
    if latent:
        kk_ref[0:BLOCK] = kprev_ref[...]
        kk_ref[BLOCK:BLOCK + tq] = kcur_ref[...]
        kk_ref[BLOCK + tq:] = knext_ref[...]
        vv_ref[0:BLOCK] = vprev_ref[...]
        vv_ref[BLOCK:BLOCK + tq] = vcur_ref[...]
        vv_ref[BLOCK + tq:] = vnext_ref[...]
    cat_ref[:, 0:CONV_DIM] = c_ref[...]

    rows = 4 * BLOCK
    lane = lax.broadcasted_iota(jnp.int32, (2 * BLOCK, LANES), 1)
    row = lax.broadcasted_iota(jnp.int32, (rows, 1), 0)
    qrow = lax.broadcasted_iota(jnp.int32, (rows, 3 * BLOCK), 0) % BLOCK
    kcol = lax.broadcasted_iota(jnp.int32, (rows, 3 * BLOCK), 1)
    band = jnp.abs(kcol - BLOCK - qrow) <= WINDOW
    for n in range(tq // BLOCK):
        valid = None
        if latent:
            lo = 0 if n > 0 else jnp.where(i > 0, 0, BLOCK)
            hi = 3 * BLOCK if n < tq // BLOCK - 1 else jnp.where(i < n_tiles - 1, 3 * BLOCK, 2 * BLOCK)
            valid = band & (kcol >= lo) & (kcol < hi)
        for hk in range(ATT_KV_HEADS):
            kvl = slice(hk * LANES, (hk + 1) * LANES)
            q2 = jnp.concatenate(
                [q_ref[n * BLOCK:(n + 1) * BLOCK, (2 * hk + p) * LANES:(2 * hk + p + 1) * LANES] for p in range(2)],
                axis=0)
            kloc = vloc = None
            if latent:
                kloc = kk_ref[n * BLOCK:(n + 3) * BLOCK, kvl]
                vloc = vv_ref[n * BLOCK:(n + 3) * BLOCK, kvl]
            zero = jnp.zeros_like(q2)
            q4 = jnp.concatenate([jnp.where(lane < HEAD_DIM, q2, zero), jnp.where(lane >= HEAD_DIM, q2, zero)], axis=0)
            sinks = [sink_ref[hk * 4 + 2 * p + e] * LOG2E for e in range(2) for p in range(2)]
            sink_col = jnp.where(row < 2 * BLOCK, jnp.where(row < BLOCK, sinks[0], sinks[1]),
                                 jnp.where(row < 3 * BLOCK, sinks[2], sinks[3]))
            o4 = _attend(q4, kloc, vloc, kctx_ref[:, kvl], vctx_ref[:, kvl], sink_col, valid)
            o2 = jnp.where(lane < HEAD_DIM, o4[:2 * BLOCK], o4[2 * BLOCK:]).astype(BF16)
            for p in range(2):
                c0 = CONV_DIM + (2 * hk + p) * LANES
                cat_ref[n * BLOCK:(n + 1) * BLOCK, c0:c0 + LANES] = o2[p * BLOCK:(p + 1) * BLOCK]
    y = _dot(cat_ref[...], wout_ref[...])
    o_ref[...] = _post_norm(h_ref[...], mod_ref[5:6] * y, g_ref[...], b_ref[...])


def _even_mix(h, mod, row_of_batch, conv, q, k, v, kctx, vctx, sink, w_out, ln_g, ln_b, layer, j, latent, tq):
    bsz, n, _ = h.shape
    n_tiles = n // tq
    kb = tq // BLOCK

    def tile(width):
        return pl.BlockSpec((None, tq, width), lambda b, t: (b, t, 0))

    def prev(rows, per_tile, width):
        return pl.BlockSpec((None, rows, width), lambda b, t: (b, jnp.maximum(t * per_tile - 1, 0), 0))

    def nxt(rows, per_tile, width):
        last = n // rows - 1
        return pl.BlockSpec((None, rows, width), lambda b, t: (b, jnp.minimum((t + 1) * per_tile, last), 0))

    ctx_kv = pl.BlockSpec((None, CTX_LEN, 2 * KV_DIM), lambda b, t: (b, 0, 0))
    in_specs = [pl.BlockSpec(memory_space=pltpu.SMEM), tile(D_MODEL), _mod_spec(layer, row_of_batch),
                tile(CONV_DIM), tile(ATT_DIM)]
    args = [sink, h, mod, conv, q]
    if latent:
        in_specs += [prev(BLOCK, kb, 2 * KV_DIM), tile(2 * KV_DIM), nxt(BLOCK, kb, 2 * KV_DIM),
                     prev(BLOCK, kb, 2 * KV_DIM), tile(2 * KV_DIM), nxt(BLOCK, kb, 2 * KV_DIM)]
        args += [k, k, k, v, v, v]
    in_specs += [ctx_kv, ctx_kv, _resident((None, MIX_EVEN, D_MODEL), lambda b, t: (j, 0, 0)),
                 _ln_spec(layer, 1), _ln_spec(layer, 1)]
    args += [kctx, vctx, w_out, ln_g, ln_b]
    scratch = [pltpu.VMEM((tq + 2 * BLOCK, 2 * KV_DIM), BF16)] * 2 if latent else []
    scratch += [pltpu.VMEM((tq, MIX_EVEN), BF16)]
    return pl.pallas_call(
        functools.partial(_even_mix_kernel, latent=latent, tq=tq, n_tiles=n_tiles),
        grid=(bsz, n_tiles),
        in_specs=in_specs,
        out_specs=tile(D_MODEL),
        out_shape=jax.ShapeDtypeStruct(h.shape, F32),
        scratch_shapes=scratch,
        compiler_params=_params("parallel", "parallel"),
        name="even_mix",
    )(*args)


RET_COLS = 2 * RET_QK + 2 * RET_V


def _rope256(x, cos, sin):
    half = RET_DK // 2
    parts = []
    for hd in range(RET_HEADS):
        x1 = x[:, hd * RET_DK:hd * RET_DK + half]
        x2 = x[:, hd * RET_DK + half:(hd + 1) * RET_DK]
        parts += [x1 * cos - x2 * sin, x2 * cos + x1 * sin]
    return jnp.concatenate(parts, axis=1)


def _ret_proj_kernel(*refs, rope):
    if rope:
        h_ref, mod_ref, w_ref, cos_ref, sin_ref, q_ref, k_ref, v_ref, sg_ref = refs
    else:
        h_ref, mod_ref, w_ref, q_ref, k_ref, v_ref, sg_ref = refs
    u = (h_ref[...] * (1.0 + mod_ref[4:5]) + mod_ref[3:4]).astype(BF16)
    q = _dot(u, w_ref[:, 0:RET_QK])
    k = _dot(u, w_ref[:, RET_QK:2 * RET_QK])
    if rope:
        cos, sin = cos_ref[...], sin_ref[...]
        q = _rope256(q, cos, sin)
        k = _rope256(k, cos, sin)
    q_ref[...] = (q * (RET_DK ** -0.5)).astype(BF16)
    k_ref[...] = k.astype(BF16)
    v_ref[...] = _dot(u, w_ref[:, 2 * RET_QK:2 * RET_QK + RET_V]).astype(BF16)
    sg_ref[...] = _silu(_dot(u, w_ref[:, 2 * RET_QK + RET_V:RET_COLS])).astype(BF16)


def _ret_proj(h, mod, row_of_batch, w, layer, j, rope_tables, tm):
    bsz, n, _ = h.shape
    rope = rope_tables is not None
    in_specs = [
        pl.BlockSpec((None, tm, D_MODEL), lambda b, t: (b, t, 0)),
        _mod_spec(layer, row_of_batch),
        _resident((None, D_MODEL, RET_COLS), lambda b, t: (j, 0, 0)),
    ]
    args = [h, mod, w]
    if rope:
        in_specs += [pl.BlockSpec((tm, LANES), lambda b, t: (t, 0))] * 2
        args += list(rope_tables)
    widths = [RET_QK, RET_QK, RET_V, RET_V]
    return pl.pallas_call(
        functools.partial(_ret_proj_kernel, rope=rope),
        grid=(bsz, n // tm),
        in_specs=in_specs,
        out_specs=[pl.BlockSpec((None, tm, wd), lambda b, t: (b, t, 0)) for wd in widths],
        out_shape=[jax.ShapeDtypeStruct((bsz, n, wd), BF16) for wd in widths],
        compiler_params=_params("parallel", "parallel"),
        name="ret_proj",
    )(*args)


RET_BLOCK = 256


def _ret_mix_kernel(*refs, nc, has_init, out_state):
    refs = list(refs)
    decay_ref, q_ref, k_ref, v_ref, sg_ref, h_ref, mod_ref = refs[:7]
    pos = 7
    init_ref = fin_ref = None
    if has_init:
        init_ref = refs[pos]
        pos += 1
    w_ref, g_ref, b_ref, o_ref = refs[pos:pos + 4]
    pos += 4
    if out_state:
        fin_ref = refs[pos]
        pos += 1
    state_ref, dmat_ref, qdec_ref, kdec_ref, cdec_ref, ob_ref, z_ref = refs[pos:]
    s = pl.program_id(1)
    rb = RET_BLOCK

    def load_state(d):
        for hd in range(RET_HEADS):
            state_ref[hd] = init_ref[d, hd] if has_init else jnp.zeros((RET_DK, RET_DV), F32)

    @pl.when(s == 0)
    def _():
        prow = lax.broadcasted_iota(jnp.int32, (rb, rb), 0)
        pcol = lax.broadcasted_iota(jnp.int32, (rb, rb), 1)
        posf = lax.broadcasted_iota(jnp.int32, (rb, LANES), 0).astype(F32)
        for d in range(2):
            rel = (pcol - prow) if d else (prow - pcol)
            keep = (rel > 0) if d else (rel >= 0)
            relf = jnp.where(keep, rel, 0).astype(F32)
            qpow = (rb - posf) if d else (posf + 1.0)
            kpow = posf if d else (rb - 1.0 - posf)
            for hd in range(RET_HEADS):
                def log_gamma(shape):
                    return -jnp.log(1.0 + jnp.exp(-jnp.full(shape, decay_ref[d, hd], F32)))

                dmat_ref[d, hd] = jnp.where(keep, jnp.exp(log_gamma((rb, rb)) * relf), 0.0)
                qdec_ref[d, hd] = jnp.exp(log_gamma((rb, LANES)) * qpow)
                kdec_ref[d, hd] = jnp.exp(log_gamma((rb, LANES)) * kpow)
                cdec_ref[d, hd] = jnp.exp(log_gamma((8, LANES)) * float(rb))
        load_state(1)

    @pl.when(s == nc)
    def _():
        if out_state:
            fin_ref[1] = state_ref[...]
        load_state(0)

    def lanes(t, width):
        return jnp.concatenate([t] * (width // LANES), axis=1)

    def head_out(d, hd):
        qh = q_ref[:, hd * RET_DK:(hd + 1) * RET_DK]
        kh = k_ref[:, hd * RET_DK:(hd + 1) * RET_DK]
        vh = v_ref[:, hd * RET_DV:(hd + 1) * RET_DV]
        st = state_ref[hd]
        sc = _dot_nt(qh, kh) * dmat_ref[d, hd]
        o = _dot(sc.astype(BF16), vh) + _dot((qh * lanes(qdec_ref[d, hd], RET_DK)).astype(BF16), st.astype(BF16))
        kd = (kh * lanes(kdec_ref[d, hd], RET_DK)).astype(BF16)
        state_ref[hd] = st * lanes(cdec_ref[d, hd, 0:1, :], RET_DV) + _dot_tn(kd, vh)
        return o

    @pl.when(s < nc)
    def _():
        row0 = pl.multiple_of((nc - 1 - s) * rb, rb)
        for hd in range(RET_HEADS):
            ob_ref[pl.ds(row0, rb), hd * RET_DV:(hd + 1) * RET_DV] = head_out(1, hd).astype(BF16)

    @pl.when(s >= nc)
    def _():
        row0 = pl.multiple_of((s - nc) * rb, rb)
        for hd in range(RET_HEADS):
            cols = slice(hd * RET_DV, (hd + 1) * RET_DV)
            on = _standardize(head_out(0, hd) + ob_ref[pl.ds(row0, rb), cols].astype(F32))
            z_ref[:, cols] = (sg_ref[:, cols] * on).astype(BF16)
        y = _dot(z_ref[...], w_ref[...])
        o_ref[...] = _post_norm(h_ref[...], mod_ref[5:6] * y, g_ref[...], b_ref[...])

    if out_state:
        @pl.when(s == 2 * nc - 1)
        def _():
            fin_ref[0] = state_ref[...]


def _ret_mix(decay, q, k, v, sg, h, mod, row_of_batch, init, w, ln_g, ln_b, layer, j, out_state):
    bsz, n, _ = q.shape
    rb = RET_BLOCK
    nc = n // rb
    chunk = lambda b, s: (b, jnp.where(s < nc, nc - 1 - s, s - nc), 0)
    fwd = lambda b, s: (b, jnp.maximum(s - nc, 0), 0)
    state_shape = (None, 2, RET_HEADS, RET_DK, RET_DV)
    state_index = lambda b, s: (b, 0, 0, 0, 0)
    in_specs = [pl.BlockSpec(memory_space=pltpu.SMEM),
                pl.BlockSpec((None, rb, RET_QK), chunk),
                pl.BlockSpec((None, rb, RET_QK), chunk),
                pl.BlockSpec((None, rb, RET_V), chunk),
                pl.BlockSpec((None, rb, RET_V), fwd),
                pl.BlockSpec((None, rb, D_MODEL), fwd),
                _mod_spec(layer, row_of_batch)]
    args = [decay, q, k, v, sg, h, mod]
    if init is not None:
        in_specs.append(_resident(state_shape, state_index))
        args.append(init)
    in_specs += [_resident((None, RET_V, D_MODEL), lambda b, s: (j, 0, 0)), _ln_spec(layer, 1), _ln_spec(layer, 1)]
    args += [w, ln_g, ln_b]
    out_specs = [pl.BlockSpec((None, rb, D_MODEL), fwd)]
    out_shape = [jax.ShapeDtypeStruct(h.shape, F32)]
    if out_state:
        out_specs.append(pl.BlockSpec(state_shape, state_index))
        out_shape.append(jax.ShapeDtypeStruct((bsz, 2, RET_HEADS, RET_DK, RET_DV), F32))
    return pl.pallas_call(
        functools.partial(_ret_mix_kernel, nc=nc, has_init=init is not None, out_state=out_state),
        grid=(bsz, 2 * nc),
        in_specs=in_specs,
        out_specs=out_specs,
        out_shape=out_shape,
        scratch_shapes=[pltpu.VMEM((RET_HEADS, RET_DK, RET_DV), F32),
                        pltpu.VMEM((2, RET_HEADS, rb, rb), F32),
                        pltpu.VMEM((2, RET_HEADS, rb, LANES), F32),
                        pltpu.VMEM((2, RET_HEADS, rb, LANES), F32),
                        pltpu.VMEM((2, RET_HEADS, 8, LANES), F32),
                        pltpu.VMEM((n, RET_V), BF16),
                        pltpu.VMEM((rb, RET_V), BF16)],
        compiler_params=_params("parallel", "arbitrary"),
        name="ret_mix",
    )(*args)


def _axial_rope_tables(n_tokens):
    rows = n_tokens // GRID_W
    r = jnp.broadcast_to(jnp.arange(rows, dtype=F32)[:, None], (rows, GRID_W)).reshape(-1)
    cidx = jnp.broadcast_to(jnp.arange(GRID_W, dtype=F32)[None, :], (rows, GRID_W)).reshape(-1)
    nf = HEAD_DIM // 4
    inv = ROPE_BASE ** (-jnp.arange(nf, dtype=F32) / nf)
    ang = jnp.concatenate([r[:, None] * inv[None], cidx[:, None] * inv[None]], axis=-1)
    cos, sin = jnp.cos(ang), jnp.sin(ang)
    reps = LANES // HEAD_DIM
    return jnp.tile(jnp.concatenate([cos, cos], axis=-1), (1, reps)), jnp.tile(jnp.concatenate([-sin, sin], axis=-1), (1, reps))


def _retention_rope_tables(n_tokens):
    inv = 1.0 / (ROPE_BASE ** jnp.linspace(0.0, 1.0, RET_DK // 2, dtype=F32))
    ang = jnp.arange(n_tokens, dtype=F32)[:, None] * inv[None]
    return jnp.cos(ang), jnp.sin(ang)


def _even_weight_layout(ev_w_in):
    def dup(cols):
        heads = [cols[..., hd * HEAD_DIM:(hd + 1) * HEAD_DIM] for hd in range(ATT_KV_HEADS)]
        return jnp.concatenate([t for hd in heads for t in (hd, hd)], axis=-1)

    return jnp.concatenate([ev_w_in[..., :KV_OFF], dup(ev_w_in[..., KV_OFF:KV_OFF + KV_DIM]),
                            dup(ev_w_in[..., KV_OFF + KV_DIM:])], axis=-1).astype(BF16)


def kernel(x, c, ctx, c_ctx, w_ada, b_ada, ln_g, ln_b, ffn_w_in, ffn_w_out, ev_w_in, ev_conv_w, ev_conv_b,
           ev_norm_g, ev_norm_b, ev_sink, ev_w_out, ret_w_in, ret_decay, ret_w_out):
    bsz, n_tok, _ = x.shape
    assert bsz < MOD_ROWS and ctx.shape[1] == CTX_LEN
    rope_a = _axial_rope_tables(n_tok)
    rope_r = _retention_rope_tables(n_tok)
    cvec = jnp.zeros((MOD_ROWS, D_MODEL), F32).at[:bsz].set(c).at[bsz].set(c_ctx)
    mod = _modulation(cvec, w_ada, b_ada)
    lat_row = lambda b: b
    ctx_row = lambda b: bsz

    ffn_in = ffn_w_in.astype(BF16)
    ffn_out = ffn_w_out.astype(BF16)
    ev_in = _even_weight_layout(ev_w_in)
    ev_out = ev_w_out.astype(BF16)
    ret_in = ret_w_in.astype(BF16)
    ret_out = ret_w_out.astype(BF16)
    lng = ln_g.reshape(DEPTH * 3, 1, D_MODEL)
    lnb = ln_b.reshape(DEPTH * 3, 1, D_MODEL)
    conv_b = ev_conv_b[:, None, :]
    conv_g = ev_norm_g[:, None, :]
    conv_beta = ev_norm_b[:, None, :]

    tm_lat = min(512, n_tok)
    tm_ctx = CTX_LEN
    tq_lat = min(512, n_tok)

    def ffn(hh, row, tm, layer, j):
        return _ffn_half(hh, mod, row, ffn_in, ffn_out, lng, lnb, layer, j, 6 * j, tm)

    def ffn_ctx(hh, layer, j):
        flat = hh.reshape(1, bsz * CTX_LEN, D_MODEL)
        return ffn(flat, ctx_row, min(tm_lat, bsz * CTX_LEN), layer, j).reshape(hh.shape)

    h, hc = x, ctx
    for i in range(DEPTH):
        last = i == DEPTH - 1
        j = i // 2
        h = ffn(h, lat_row, tm_lat, i, 0)
        hc = ffn_ctx(hc, i, 0)
        if i % 2 == 0:
            proj = functools.partial(_even_proj, w=ev_in, cw=ev_conv_w, cb=conv_b, cg=conv_g, cbeta=conv_beta,
                                     layer=i, j=j)
            yc, qc, kc, vc = proj(hc, mod, ctx_row, rope_tables=None, tm=tm_ctx)
            yl, ql, kl, vl = proj(h, mod, lat_row, rope_tables=rope_a, tm=tm_lat)
            mix = functools.partial(_even_mix, kctx=kc, vctx=vc, sink=ev_sink[j], w_out=ev_out, ln_g=lng, ln_b=lnb,
                                    layer=i, j=j)
            h = mix(h, mod, lat_row, yl, ql, kl, vl, latent=True, tq=tq_lat)
            if not last:
                hc = mix(hc, mod, ctx_row, yc, qc, None, None, latent=False, tq=CTX_LEN)
        else:
            decay = ret_decay[j]
            flat = hc.reshape(1, bsz * CTX_LEN, D_MODEL)
            qc, kc, vc, sgc = [t.reshape(bsz, CTX_LEN, -1) for t in _ret_proj(
                flat, mod, ctx_row, ret_in, i, j, None, min(tm_lat, bsz * CTX_LEN))]
            ql, kl, vl, sgl = _ret_proj(h, mod, lat_row, ret_in, i, j, rope_r, tm_lat)
            hc_next, states = _ret_mix(decay, qc, kc, vc, sgc, hc, mod, ctx_row, None, ret_out, lng, lnb, i, j, True)
            h, = _ret_mix(decay, ql, kl, vl, sgl, h, mod, lat_row, states, ret_out, lng, lnb, i, j, False)
            if not last:
                hc = hc_next
        h = ffn(h, lat_row, tm_lat, i, 1)
        if not last:
            hc = ffn_ctx(hc, i, 1)
    return h
```

```python
import functools

import jax
import jax.numpy as jnp
from jax import lax
from jax.experimental import pallas as pl
from jax.experimental.pallas import tpu as pltpu

F32 = jnp.float32
BF16 = jnp.bfloat16

D_MODEL = 1024
DEPTH = 4
GRID_W = 64
CTX_LEN = 256
CONV_DIM = 512
CONV_WIDTH = 31
CONV_HALO = 16
ATT_HEADS = 8
ATT_KV_HEADS = 2
HEAD_DIM = 64
ATT_DIM = ATT_HEADS * HEAD_DIM
KV_DIM = ATT_KV_HEADS * HEAD_DIM
WINDOW = 128
BLOCK = 128
ATT_SCALE = HEAD_DIM ** -0.5
ROPE_BASE = 10000.0
KV_OFF = 2 * CONV_DIM + ATT_DIM
MIX_EVEN = CONV_DIM + ATT_DIM
RET_HEADS = 4
RET_DK = 256
RET_DV = 512
RET_QK = RET_HEADS * RET_DK
RET_V = RET_HEADS * RET_DV
D_FF = 2816
DEEPNORM_ALPHA = (2 * DEPTH) ** 0.25
LN_EPS = 1e-5
NEG_INF = -1e30

LANES = 128
SUBLANES = 8
MOD_ROWS = 16
VMEM_LIMIT_BYTES = 56 * 1024 * 1024
FFN_CHUNK = 256


def _params(*sem):
    return pltpu.CompilerParams(dimension_semantics=sem, vmem_limit_bytes=VMEM_LIMIT_BYTES)


def _dot(a, b):
    return jnp.dot(a, b, preferred_element_type=F32)


def _dot_nt(a, b):
    return lax.dot_general(a, b, (((1,), (1,)), ((), ())), preferred_element_type=F32)


def _dot_tn(a, b):
    return lax.dot_general(a, b, (((0,), (0,)), ((), ())), preferred_element_type=F32)


def _silu(x):
    return x * jax.nn.sigmoid(x)


def _standardize(x):
    mu = jnp.mean(x, axis=-1, keepdims=True)
    d = x - mu
    var = jnp.mean(d * d, axis=-1, keepdims=True)
    return d * lax.rsqrt(var + LN_EPS)


def _post_norm(h, out, g, b):
    return _standardize(DEEPNORM_ALPHA * h + out) * g + b


def _resident(shape, index_map):
    return pl.BlockSpec(shape, index_map, pipeline_mode=pl.Buffered(1))


def _mod_kernel(c_ref, w_ref, b_ref, o_ref):
    s = _silu(c_ref[...]).astype(BF16)
    o_ref[0] = _dot(s, w_ref[0].astype(BF16)) + b_ref[0]


def _modulation(cvec, w_ada, b_ada):
    tn = 1536
    n_out = 9 * D_MODEL
    out = pl.pallas_call(
        _mod_kernel,
        grid=(DEPTH, n_out // tn),
        in_specs=[
            pl.BlockSpec((MOD_ROWS, D_MODEL), lambda i, n: (0, 0)),
            pl.BlockSpec((1, D_MODEL, tn), lambda i, n: (i, 0, n)),
            pl.BlockSpec((1, 1, tn), lambda i, n: (i, 0, n)),
        ],
        out_specs=pl.BlockSpec((1, MOD_ROWS, tn), lambda i, n: (i, 0, n)),
        out_shape=jax.ShapeDtypeStruct((DEPTH, MOD_ROWS, n_out), F32),
        compiler_params=_params("parallel", "parallel"),
        name="adaln_modulation",
    )(cvec, w_ada, b_ada.reshape(DEPTH, 1, n_out))
    return out.reshape(DEPTH, MOD_ROWS, 9, D_MODEL)


def _mod_spec(layer, row_of_batch):
    return pl.BlockSpec((None, None, 9, D_MODEL), lambda b, t: (layer, row_of_batch(b), 0, 0))


def _ln_spec(layer, j):
    return pl.BlockSpec((None, 1, D_MODEL), lambda b, t: (layer * 3 + j, 0, 0))


def _ffn_kernel(h_ref, mod_ref, win_ref, wout_ref, g_ref, b_ref, o_ref, act_ref, *, k0):
    h = h_ref[...]
    shift, scale, gate = mod_ref[k0:k0 + 1], mod_ref[k0 + 1:k0 + 2], mod_ref[k0 + 2:k0 + 3]
    u = (h * (1.0 + scale) + shift).astype(BF16)
    for n in range(D_FF // FFN_CHUNK):
        lo = n * FFN_CHUNK
        gt = _dot(u, win_ref[:, lo:lo + FFN_CHUNK])
        up = _dot(u, win_ref[:, D_FF + lo:D_FF + lo + FFN_CHUNK])
        act_ref[:, lo:lo + FFN_CHUNK] = (_silu(gt) * up).astype(BF16)
    y = _dot(act_ref[...], wout_ref[...])
    o_ref[...] = _post_norm(h, (0.5 * gate) * y, g_ref[...], b_ref[...])


def _ffn_half(h, mod, row_of_batch, w_in, w_out, ln_g, ln_b, layer, j, k0, tm):
    bsz, n, _ = h.shape
    return pl.pallas_call(
        functools.partial(_ffn_kernel, k0=k0),
        grid=(bsz, n // tm),
        in_specs=[
            pl.BlockSpec((None, tm, D_MODEL), lambda b, t: (b, t, 0)),
            _mod_spec(layer, row_of_batch),
            _resident((None, None, D_MODEL, 2 * D_FF), lambda b, t: (layer, j, 0, 0)),
            _resident((None, None, D_FF, D_MODEL), lambda b, t: (layer, j, 0, 0)),
            _ln_spec(layer, 2 * j),
            _ln_spec(layer, 2 * j),
        ],
        out_specs=pl.BlockSpec((None, tm, D_MODEL), lambda b, t: (b, t, 0)),
        out_shape=jax.ShapeDtypeStruct(h.shape, F32),
        scratch_shapes=[pltpu.VMEM((tm, D_FF), BF16)],
        compiler_params=_params("parallel", "parallel"),
        name="ffn_half",
    )(h, mod, w_in, w_out, ln_g, ln_b)


EVEN_COLS = 2 * CONV_DIM + ATT_DIM + 4 * KV_DIM


def _rope64(x, cos, sin):
    width = x.shape[-1]
    reps = width // LANES
    cosw = jnp.concatenate([cos] * reps, axis=1) if reps > 1 else cos
    sinw = jnp.concatenate([sin] * reps, axis=1) if reps > 1 else sin
    lane = lax.broadcasted_iota(jnp.int32, x.shape, 1)
    first_half = (lane % HEAD_DIM) < (HEAD_DIM // 2)
    partner = jnp.where(first_half, pltpu.roll(x, width - HEAD_DIM // 2, 1), pltpu.roll(x, HEAD_DIM // 2, 1))
    return x * cosw + partner * sinw


CONV_ROWS = 64
CONV_SHIFT_PAD = (CONV_HALO - CONV_WIDTH // 2 + CONV_WIDTH - 1) // SUBLANES * SUBLANES
CONV_GROUP = 256
LOG2E = 1.4426950408889634


def _conv_taps(ybuf_ref, ysh_ref, conv_ref, cw_ref, cb_ref, cols, tm):
    for r in range(1, SUBLANES):
        ysh_ref[r - 1, :, cols] = ybuf_ref[r:r + tm + CONV_SHIFT_PAD, cols]
    first = CONV_HALO - CONV_WIDTH // 2
    for c in range(cols.start // LANES, cols.stop // LANES):
        cl = slice(c * LANES, (c + 1) * LANES)
        for r0 in range(0, tm, CONV_ROWS):
            acc = jnp.zeros((CONV_ROWS, LANES), F32) + cb_ref[:, cl]
            for j in range(CONV_WIDTH):
                a, r = divmod(first + j, SUBLANES)
                lo = r0 + a * SUBLANES
                window = ysh_ref[r - 1, lo:lo + CONV_ROWS, cl] if r else ybuf_ref[lo:lo + CONV_ROWS, cl]
                acc = acc + window * cw_ref[j:j + 1, cl]
            conv_ref[r0:r0 + CONV_ROWS, cl] = acc


def _even_proj_kernel(*refs, rope, tm, n_tiles):
    refs = list(refs)
    hprev_ref, h_ref, hnext_ref, mod_ref, w_ref = refs[:5]
    pos = 5
    if rope:
        cos_ref, sin_ref = refs[pos:pos + 2]
        pos += 2
    (cw_ref, cb_ref, cg_ref, cbeta_ref, cq_ref, kv_ref, ybuf_ref, ysh_ref, conv_ref) = refs[pos:]
    t = pl.program_id(1)

    def modulated(x):
        return (x * (1.0 + mod_ref[4:5]) + mod_ref[3:4]).astype(BF16)

    u = modulated(h_ref[...])
    u_ext = jnp.concatenate([modulated(hprev_ref[...]), u, modulated(hnext_ref[...])], axis=0)
    erow = lax.broadcasted_iota(jnp.int32, (tm + 2 * CONV_HALO, 1), 0)
    inside = (erow >= jnp.where(t > 0, 0, CONV_HALO)) & (erow < jnp.where(t < n_tiles - 1, tm + 2 * CONV_HALO, tm + CONV_HALO))
    for g0 in range(0, CONV_DIM, CONV_GROUP):
        cols = slice(g0, g0 + CONV_GROUP)
        a = _dot(u_ext, w_ref[:, g0:g0 + CONV_GROUP])
        gt = _dot(u_ext, w_ref[:, CONV_DIM + g0:CONV_DIM + g0 + CONV_GROUP])
        ybuf_ref[:, cols] = jnp.where(inside, a * jax.nn.sigmoid(gt), 0.0)
        _conv_taps(ybuf_ref, ysh_ref, conv_ref, cw_ref, cb_ref, cols, tm)
    cq_ref[:, 0:CONV_DIM] = _silu(_standardize(conv_ref[...]) * cg_ref[...] + cbeta_ref[...]).astype(BF16)
    q = _dot(u, w_ref[:, 2 * CONV_DIM:KV_OFF])
    k = _dot(u, w_ref[:, KV_OFF:KV_OFF + 2 * KV_DIM])
    v = _dot(u, w_ref[:, KV_OFF + 2 * KV_DIM:EVEN_COLS])
    if rope:
        cos, sin = cos_ref[...], sin_ref[...]
        q = _rope64(q, cos, sin)
        k = _rope64(k, cos, sin)
    cq_ref[:, CONV_DIM:] = (q * (ATT_SCALE * LOG2E)).astype(BF16)
    kv_ref[:, 0:2 * KV_DIM] = k.astype(BF16)
    kv_ref[:, 2 * KV_DIM:] = v.astype(BF16)


def _even_proj(h, mod, row_of_batch, w, cw, cb, cg, cbeta, layer, j, rope_tables, tm):
    bsz, n, _ = h.shape
    n_tiles = n // tm
    hb = tm // CONV_HALO
    rope = rope_tables is not None
    small = lambda rows: pl.BlockSpec((None, rows, CONV_DIM), lambda b, t: (j, 0, 0))
    in_specs = [
        pl.BlockSpec((None, CONV_HALO, D_MODEL), lambda b, t: (b, jnp.maximum(t * hb - 1, 0), 0)),
        pl.BlockSpec((None, tm, D_MODEL), lambda b, t: (b, t, 0)),
        pl.BlockSpec((None, CONV_HALO, D_MODEL), lambda b, t: (b, jnp.minimum((t + 1) * hb, n // CONV_HALO - 1), 0)),
        _mod_spec(layer, row_of_batch),
        _resident((None, D_MODEL, EVEN_COLS), lambda b, t: (j, 0, 0)),
    ]
    args = [h, h, h, mod, w]
    if rope:
        in_specs += [pl.BlockSpec((tm, LANES), lambda b, t: (t, 0))] * 2
        args += list(rope_tables)
    in_specs += [small(CONV_WIDTH), small(1), small(1), small(1)]
    args += [cw, cb, cg, cbeta]

    def out(width):
        return (pl.BlockSpec((None, tm, width), lambda b, t: (b, t, 0)),
                jax.ShapeDtypeStruct((bsz, n, width), BF16))

    outs = [out(MIX_EVEN), out(4 * KV_DIM)]
    return pl.pallas_call(
        functools.partial(_even_proj_kernel, rope=rope, tm=tm, n_tiles=n_tiles),
        grid=(bsz, n_tiles),
        in_specs=in_specs,
        out_specs=[o[0] for o in outs],
        out_shape=[o[1] for o in outs],
        scratch_shapes=[pltpu.VMEM((tm + 2 * CONV_HALO, CONV_DIM), F32),
                        pltpu.VMEM((SUBLANES - 1, tm + CONV_SHIFT_PAD, CONV_DIM), F32),
                        pltpu.VMEM((tm, CONV_DIM), F32)],
        compiler_params=_params("parallel", "parallel"),
        name="even_proj",
    )(*args)


def _attend(qe, kloc, vloc, kctx, vctx, sink_col, valid):
    s_ctx = _dot_nt(qe, kctx)
    m = jnp.maximum(jnp.max(s_ctx, axis=-1, keepdims=True), sink_col)
    if kloc is not None:
        s_loc = jnp.where(valid, _dot_nt(qe, kloc), NEG_INF)
        m = jnp.maximum(m, jnp.max(s_loc, axis=-1, keepdims=True))
        p_loc = jnp.exp2(s_loc - m)
    p_ctx = jnp.exp2(s_ctx - m)
    den = jnp.sum(p_ctx, axis=-1, keepdims=True) + jnp.exp2(sink_col - m)
    o = _dot(p_ctx.astype(BF16), vctx)
    if kloc is not None:
        den = den + jnp.sum(p_loc, axis=-1, keepdims=True)
        o = o + _dot(p_loc.astype(BF16), vloc)
    return o / den


def _even_mix_kernel(*refs, latent, tq, n_tiles):
    if latent:
        (sink_ref, h_ref, mod_ref, cq_ref, kvprev_ref, kvcur_ref, kvnext_ref, kvctx_ref,
         wout_ref, g_ref, b_ref, o_ref, kvs_ref, cat_ref) = refs
    else:
        (sink_ref, h_ref, mod_ref, cq_ref, kvctx_ref, wout_ref, g_ref, b_ref, o_ref, cat_ref) = refs
    i = pl.program_id(1)
    if latent:
        kvs_ref[0:BLOCK] = kvprev_ref[...]
        kvs_ref[BLOCK:BLOCK + tq] = kvcur_ref[...]
        kvs_ref[BLOCK + tq:] = kvnext_ref[...]
    cat_ref[:, 0:CONV_DIM] = cq_ref[:, 0:CONV_DIM]

    rows = 4 * BLOCK
    lane = lax.broadcasted_iota(jnp.int32, (2 * BLOCK, LANES), 1)
    row = lax.broadcasted_iota(jnp.int32, (rows, 1), 0)
    qrow = lax.broadcasted_iota(jnp.int32, (rows, 3 * BLOCK), 0) % BLOCK
    kcol = lax.broadcasted_iota(jnp.int32, (rows, 3 * BLOCK), 1)
    band = jnp.abs(kcol - BLOCK - qrow) <= WINDOW
    for n in range(tq // BLOCK):
        valid = None
        if latent:
            lo = 0 if n > 0 else jnp.where(i > 0, 0, BLOCK)
            hi = 3 * BLOCK if n < tq // BLOCK - 1 else jnp.where(i < n_tiles - 1, 3 * BLOCK, 2 * BLOCK)
            valid = band & (kcol >= lo) & (kcol < hi)
        for hk in range(ATT_KV_HEADS):
            kvl = slice(hk * LANES, (hk + 1) * LANES)
            vvl = slice(2 * KV_DIM + hk * LANES, 2 * KV_DIM + (hk + 1) * LANES)
            q2 = jnp.concatenate(
                [cq_ref[n * BLOCK:(n + 1) * BLOCK, CONV_DIM + (2 * hk + p) * LANES:CONV_DIM + (2 * hk + p + 1) * LANES]
                 for p in range(2)], axis=0)
            kloc = vloc = None
            if latent:
                kloc = kvs_ref[n * BLOCK:(n + 3) * BLOCK, kvl]
                vloc = kvs_ref[n * BLOCK:(n + 3) * BLOCK, vvl]
            zero = jnp.zeros_like(q2)
            q4 = jnp.concatenate([jnp.where(lane < HEAD_DIM, q2, zero), jnp.where(lane >= HEAD_DIM, q2, zero)], axis=0)
            sinks = [sink_ref[hk * 4 + 2 * p + e] * LOG2E for e in range(2) for p in range(2)]
            sink_col = jnp.where(row < 2 * BLOCK, jnp.where(row < BLOCK, sinks[0], sinks[1]),
                                 jnp.where(row < 3 * BLOCK, sinks[2], sinks[3]))
            o4 = _attend(q4, kloc, vloc, kvctx_ref[:, kvl], kvctx_ref[:, vvl], sink_col, valid)
            o2 = jnp.where(lane < HEAD_DIM, o4[:2 * BLOCK], o4[2 * BLOCK:]).astype(BF16)
            for p in range(2):
                c0 = CONV_DIM + (2 * hk + p) * LANES
                cat_ref[n * BLOCK:(n + 1) * BLOCK, c0:c0 + LANES] = o2[p * BLOCK:(p + 1) * BLOCK]
    y = _dot(cat_ref[...], wout_ref[...])
    o_ref[...] = _post_norm(h_ref[...], mod_ref[5:6] * y, g_ref[...], b_ref[...])


def _even_mix(h, mod, row_of_batch, cq, kv, kvctx, sink, w_out, ln_g, ln_b, layer, j, latent, tq):
    bsz, n, _ = h.shape
    n_tiles = n // tq
    kb = tq // BLOCK

    def tile(width):
        return pl.BlockSpec((None, tq, width), lambda b, t: (b, t, 0))

    def prev(rows, per_tile, width):
        return pl.BlockSpec((None, rows, width), lambda b, t: (b, jnp.maximum(t * per_tile - 1, 0), 0))

    def nxt(rows, per_tile, width):
        last = n // rows - 1
        return pl.BlockSpec((None, rows, width), lambda b, t: (b, jnp.minimum((t + 1) * per_tile, last), 0))

    in_specs = [pl.BlockSpec(memory_space=pltpu.SMEM), tile(D_MODEL), _mod_spec(layer, row_of_batch), tile(MIX_EVEN)]
    args = [sink, h, mod, cq]
    if latent:
        in_specs += [prev(BLOCK, kb, 4 * KV_DIM), tile(4 * KV_DIM), nxt(BLOCK, kb, 4 * KV_DIM)]
        args += [kv, kv, kv]
    in_specs += [pl.BlockSpec((None, CTX_LEN, 4 * KV_DIM), lambda b, t: (b, 0, 0)),
                 _resident((None, MIX_EVEN, D_MODEL), lambda b, t: (j, 0, 0)),
                 _ln_spec(layer, 1), _ln_spec(layer, 1)]
    args += [kvctx, w_out, ln_g, ln_b]
    scratch = [pltpu.VMEM((tq + 2 * BLOCK, 4 * KV_DIM), BF16)] if latent else []
    scratch += [pltpu.VMEM((tq, MIX_EVEN), BF16)]
    return pl.pallas_call(
        functools.partial(_even_mix_kernel, latent=latent, tq=tq, n_tiles=n_tiles),
        grid=(bsz, n_tiles),
        in_specs=in_specs,
        out_specs=tile(D_MODEL),
        out_shape=jax.ShapeDtypeStruct(h.shape, F32),
        scratch_shapes=scratch,
        compiler_params=_params("parallel", "parallel"),
        name="even_mix",
    )(*args)


RET_COLS = 2 * RET_QK + 2 * RET_V


def _rope256(x, cos, sin):
    half = RET_DK // 2
    parts = []
    for hd in range(RET_HEADS):
        x1 = x[:, hd * RET_DK:hd * RET_DK + half]
        x2 = x[:, hd * RET_DK + half:(hd + 1) * RET_DK]
        parts += [x1 * cos - x2 * sin, x2 * cos + x1 * sin]
    return jnp.concatenate(parts, axis=1)


def _ret_proj_kernel(*refs, rope):
    if rope:
        h_ref, mod_ref, w_ref, cos_ref, sin_ref, qkv_ref, sg_ref = refs
    else:
        h_ref, mod_ref, w_ref, qkv_ref, sg_ref = refs
    u = (h_ref[...] * (1.0 + mod_ref[4:5]) + mod_ref[3:4]).astype(BF16)
    q = _dot(u, w_ref[:, 0:RET_QK])
    k = _dot(u, w_ref[:, RET_QK:2 * RET_QK])
    if rope:
        cos, sin = cos_ref[...], sin_ref[...]
        q = _rope256(q, cos, sin)
        k = _rope256(k, cos, sin)
    qkv_ref[:, 0:RET_QK] = (q * (RET_DK ** -0.5)).astype(BF16)
    qkv_ref[:, RET_QK:2 * RET_QK] = k.astype(BF16)
    qkv_ref[:, 2 * RET_QK:] = _dot(u, w_ref[:, 2 * RET_QK:2 * RET_QK + RET_V]).astype(BF16)
    sg_ref[...] = _silu(_dot(u, w_ref[:, 2 * RET_QK + RET_V:RET_COLS])).astype(BF16)


def _ret_proj(h, mod, row_of_batch, w, layer, j, rope_tables, tm):
    bsz, n, _ = h.shape
    rope = rope_tables is not None
    in_specs = [
        pl.BlockSpec((None, tm, D_MODEL), lambda b, t: (b, t, 0)),
        _mod_spec(layer, row_of_batch),
        _resident((None, D_MODEL, RET_COLS), lambda b, t: (j, 0, 0)),
    ]
    args = [h, mod, w]
    if rope:
        in_specs += [pl.BlockSpec((tm, LANES), lambda b, t: (t, 0))] * 2
        args += list(rope_tables)
    widths = [2 * RET_QK + RET_V, RET_V]
    return pl.pallas_call(
        functools.partial(_ret_proj_kernel, rope=rope),
        grid=(bsz, n // tm),
        in_specs=in_specs,
        out_specs=[pl.BlockSpec((None, tm, wd), lambda b, t: (b, t, 0)) for wd in widths],
        out_shape=[jax.ShapeDtypeStruct((bsz, n, wd), BF16) for wd in widths],
        compiler_params=_params("parallel", "parallel"),
        name="ret_proj",
    )(*args)


RET_BLOCK = 256


def _ret_mix_kernel(*refs, nc, has_init, out_state):
    refs = list(refs)
    decay_ref, qkv_ref, sg_ref, h_ref, mod_ref = refs[:5]
    pos = 5
    init_ref = fin_ref = None
    if has_init:
        init_ref = refs[pos]
        pos += 1
    w_ref, g_ref, b_ref, o_ref = refs[pos:pos + 4]
    pos += 4
    if out_state:
        fin_ref = refs[pos]
        pos += 1
    state_ref, dmat_ref, qdec_ref, kdec_ref, cdec_ref, ob_ref, z_ref = refs[pos:]
    s = pl.program_id(1)
    rb = RET_BLOCK

    def load_state(d):
        for hd in range(RET_HEADS):
            state_ref[hd] = init_ref[d, hd] if has_init else jnp.zeros((RET_DK, RET_DV), F32)

    @pl.when(s == 0)
    def _():
        prow = lax.broadcasted_iota(jnp.int32, (rb, rb), 0)
        pcol = lax.broadcasted_iota(jnp.int32, (rb, rb), 1)
        posf = lax.broadcasted_iota(jnp.int32, (rb, LANES), 0).astype(F32)
        for d in range(2):
            rel = (pcol - prow) if d else (prow - pcol)
            keep = (rel > 0) if d else (rel >= 0)
            relf = jnp.where(keep, rel, 0).astype(F32)
            qpow = (rb - posf) if d else (posf + 1.0)
            kpow = posf if d else (rb - 1.0 - posf)
            for hd in range(RET_HEADS):
                def log_gamma(shape):
                    return -jnp.log(1.0 + jnp.exp(-jnp.full(shape, decay_ref[d, hd], F32)))

                dmat_ref[d, hd] = jnp.where(keep, jnp.exp(log_gamma((rb, rb)) * relf), 0.0)
                qdec_ref[d, hd] = jnp.exp(log_gamma((rb, LANES)) * qpow)
                kdec_ref[d, hd] = jnp.exp(log_gamma((rb, LANES)) * kpow)
                cdec_ref[d, hd] = jnp.exp(log_gamma((8, LANES)) * float(rb))
        load_state(1)

    @pl.when(s == nc)
    def _():
        if out_state:
            fin_ref[1] = state_ref[...]
        load_state(0)

    def lanes(t, width):
        return jnp.concatenate([t] * (width // LANES), axis=1)

    def head_out(d, hd):
        qh = qkv_ref[:, hd * RET_DK:(hd + 1) * RET_DK]
        kh = qkv_ref[:, RET_QK + hd * RET_DK:RET_QK + (hd + 1) * RET_DK]
        vh = qkv_ref[:, 2 * RET_QK + hd * RET_DV:2 * RET_QK + (hd + 1) * RET_DV]
        st = state_ref[hd]
        sc = _dot_nt(qh, kh) * dmat_ref[d, hd]
        o = _dot(sc.astype(BF16), vh) + _dot((qh * lanes(qdec_ref[d, hd], RET_DK)).astype(BF16), st.astype(BF16))
        kd = (kh * lanes(kdec_ref[d, hd], RET_DK)).astype(BF16)
        state_ref[hd] = st * lanes(cdec_ref[d, hd, 0:1, :], RET_DV) + _dot_tn(kd, vh)
        return o

    @pl.when(s < nc)
    def _():
        row0 = pl.multiple_of((nc - 1 - s) * rb, rb)
        for hd in range(RET_HEADS):
            ob_ref[pl.ds(row0, rb), hd * RET_DV:(hd + 1) * RET_DV] = head_out(1, hd).astype(BF16)

    @pl.when(s >= nc)
    def _():
        row0 = pl.multiple_of((s - nc) * rb, rb)
        for hd in range(RET_HEADS):
            cols = slice(hd * RET_DV, (hd + 1) * RET_DV)
            on = _standardize(head_out(0, hd) + ob_ref[pl.ds(row0, rb), cols].astype(F32))
            z_ref[:, cols] = (sg_ref[:, cols] * on).astype(BF16)
        y = _dot(z_ref[...], w_ref[...])
        o_ref[...] = _post_norm(h_ref[...], mod_ref[5:6] * y, g_ref[...], b_ref[...])

    if out_state:
        @pl.when(s == 2 * nc - 1)
        def _():
            fin_ref[0] = state_ref[...]


def _ret_mix(decay, qkv, sg, h, mod, row_of_batch, init, w, ln_g, ln_b, layer, j, out_state):
    bsz, n, _ = h.shape
    rb = RET_BLOCK
    nc = n // rb
    chunk = lambda b, s: (b, jnp.where(s < nc, nc - 1 - s, s - nc), 0)
    fwd = lambda b, s: (b, jnp.maximum(s - nc, 0), 0)
    state_shape = (None, 2, RET_HEADS, RET_DK, RET_DV)
    state_index = lambda b, s: (b, 0, 0, 0, 0)
    in_specs = [pl.BlockSpec(memory_space=pltpu.SMEM),
                pl.BlockSpec((None, rb, 2 * RET_QK + RET_V), chunk),
                pl.BlockSpec((None, rb, RET_V), fwd),
                pl.BlockSpec((None, rb, D_MODEL), fwd),
                _mod_spec(layer, row_of_batch)]
    args = [decay, qkv, sg, h, mod]
    if init is not None:
        in_specs.append(_resident(state_shape, state_index))
        args.append(init)
    in_specs += [_resident((None, RET_V, D_MODEL), lambda b, s: (j, 0, 0)), _ln_spec(layer, 1), _ln_spec(layer, 1)]
    args += [w, ln_g, ln_b]
    out_specs = [pl.BlockSpec((None, rb, D_MODEL), fwd)]
    out_shape = [jax.ShapeDtypeStruct(h.shape, F32)]
    if out_state:
        out_specs.append(pl.BlockSpec(state_shape, state_index))
        out_shape.append(jax.ShapeDtypeStruct((bsz, 2, RET_HEADS, RET_DK, RET_DV), F32))
    return pl.pallas_call(
        functools.partial(_ret_mix_kernel, nc=nc, has_init=init is not None, out_state=out_state),
        grid=(bsz, 2 * nc),
        in_specs=in_specs,
        out_specs=out_specs,
        out_shape=out_shape,
        scratch_shapes=[pltpu.VMEM((RET_HEADS, RET_DK, RET_DV), F32),
                        pltpu.VMEM((2, RET_HEADS, rb, rb), F32),
                        pltpu.VMEM((2, RET_HEADS, rb, LANES), F32),
                        pltpu.VMEM((2, RET_HEADS, rb, LANES), F32),
                        pltpu.VMEM((2, RET_HEADS, 8, LANES), F32),
                        pltpu.VMEM((n, RET_V), BF16),
                        pltpu.VMEM((rb, RET_V), BF16)],
        compiler_params=_params("parallel", "arbitrary"),
        name="ret_mix",
    )(*args)


def _axial_rope_tables(n_tokens):
    rows = n_tokens // GRID_W
    r = jnp.broadcast_to(jnp.arange(rows, dtype=F32)[:, None], (rows, GRID_W)).reshape(-1)
    cidx = jnp.broadcast_to(jnp.arange(GRID_W, dtype=F32)[None, :], (rows, GRID_W)).reshape(-1)
    nf = HEAD_DIM // 4
    inv = ROPE_BASE ** (-jnp.arange(nf, dtype=F32) / nf)
    ang = jnp.concatenate([r[:, None] * inv[None], cidx[:, None] * inv[None]], axis=-1)
    cos, sin = jnp.cos(ang), jnp.sin(ang)
    reps = LANES // HEAD_DIM
    return jnp.tile(jnp.concatenate([cos, cos], axis=-1), (1, reps)), jnp.tile(jnp.concatenate([-sin, sin], axis=-1), (1, reps))


def _retention_rope_tables(n_tokens):
    inv = 1.0 / (ROPE_BASE ** jnp.linspace(0.0, 1.0, RET_DK // 2, dtype=F32))
    ang = jnp.arange(n_tokens, dtype=F32)[:, None] * inv[None]
    return jnp.cos(ang), jnp.sin(ang)


def _even_weight_layout(ev_w_in):
    def dup(cols):
        heads = [cols[..., hd * HEAD_DIM:(hd + 1) * HEAD_DIM] for hd in range(ATT_KV_HEADS)]
        return jnp.concatenate([t for hd in heads for t in (hd, hd)], axis=-1)

    return jnp.concatenate([ev_w_in[..., :KV_OFF], dup(ev_w_in[..., KV_OFF:KV_OFF + KV_DIM]),
                            dup(ev_w_in[..., KV_OFF + KV_DIM:])], axis=-1).astype(BF16)


def kernel(x, c, ctx, c_ctx, w_ada, b_ada, ln_g, ln_b, ffn_w_in, ffn_w_out, ev_w_in, ev_conv_w, ev_conv_b,
           ev_norm_g, ev_norm_b, ev_sink, ev_w_out, ret_w_in, ret_decay, ret_w_out):
    bsz, n_tok, _ = x.shape
    assert bsz < MOD_ROWS and ctx.shape[1] == CTX_LEN
    rope_a = _axial_rope_tables(n_tok)
    rope_r = _retention_rope_tables(n_tok)
    cvec = jnp.zeros((MOD_ROWS, D_MODEL), F32).at[:bsz].set(c).at[bsz].set(c_ctx)
    mod = _modulation(cvec, w_ada, b_ada)
    lat_row = lambda b: b
    ctx_row = lambda b: bsz

    ffn_in = ffn_w_in.astype(BF16)
    ffn_out = ffn_w_out.astype(BF16)
    ev_in = _even_weight_layout(ev_w_in)
    ev_out = ev_w_out.astype(BF16)
    ret_in = ret_w_in.astype(BF16)
    ret_out = ret_w_out.astype(BF16)
    lng = ln_g.reshape(DEPTH * 3, 1, D_MODEL)
    lnb = ln_b.reshape(DEPTH * 3, 1, D_MODEL)
    conv_b = ev_conv_b[:, None, :]
    conv_g = ev_norm_g[:, None, :]
    conv_beta = ev_norm_b[:, None, :]

    tm_lat = min(512, n_tok)
    tm_ctx = CTX_LEN
    tq_lat = min(512, n_tok)

    def ffn(hh, row, tm, layer, j):
        return _ffn_half(hh, mod, row, ffn_in, ffn_out, lng, lnb, layer, j, 6 * j, tm)

    def ffn_ctx(hh, layer, j):
        flat = hh.reshape(1, bsz * CTX_LEN, D_MODEL)
        return ffn(flat, ctx_row, min(tm_lat, bsz * CTX_LEN), layer, j).reshape(hh.shape)

    h, hc = x, ctx
    for i in range(DEPTH):
        last = i == DEPTH - 1
        j = i // 2
        h = ffn(h, lat_row, tm_lat, i, 0)
        hc = ffn_ctx(hc, i, 0)
        if i % 2 == 0:
            proj = functools.partial(_even_proj, w=ev_in, cw=ev_conv_w, cb=conv_b, cg=conv_g, cbeta=conv_beta,
                                     layer=i, j=j)
            cqc, kvc = proj(hc, mod, ctx_row, rope_tables=None, tm=tm_ctx)
            cql, kvl = proj(h, mod, lat_row, rope_tables=rope_a, tm=tm_lat)
            mix = functools.partial(_even_mix, kvctx=kvc, sink=ev_sink[j], w_out=ev_out, ln_g=lng, ln_b=lnb,
                                    layer=i, j=j)
            h = mix(h, mod, lat_row, cql, kvl, latent=True, tq=tq_lat)
            if not last:
                hc = mix(hc, mod, ctx_row, cqc, None, latent=False, tq=CTX_LEN)
        else:
            decay = ret_decay[j]
            flat = hc.reshape(1, bsz * CTX_LEN, D_MODEL)
            qkvc, sgc = [t.reshape(bsz, CTX_LEN, -1) for t in _ret_proj(
                flat, mod, ctx_row, ret_in, i, j, None, min(tm_lat, bsz * CTX_LEN))]
            qkvl, sgl = _ret_proj(h, mod, lat_row, ret_in, i, j, rope_r, tm_lat)
            hc_next, states = _ret_mix(decay, qkvc, sgc, hc, mod, ctx_row, None, ret_out, lng, lnb, i, j, True)
            h, = _ret_mix(decay, qkvl, sgl, h, mod, lat_row, states, ret_out, lng, lnb, i, j, False)
            if not last:
                hc = hc_next
        h = ffn(h, lat_row, tm_lat, i, 1)
        if not last:
            hc = ffn_ctx(hc, i, 1)
    return h
```

```python
import functools

import jax
import jax.numpy as jnp
from jax import lax
from jax.experimental import pallas as pl
from jax.experimental.pallas import tpu as pltpu

F32 = jnp.float32
BF16 = jnp.bfloat16

D_MODEL = 1024
DEPTH = 4
GRID_W = 64
CTX_LEN = 256
CONV_DIM = 512
CONV_WIDTH = 31
CONV_HALO = 16
ATT_HEADS = 8
ATT_KV_HEADS = 2
HEAD_DIM = 64
ATT_DIM = ATT_HEADS * HEAD_DIM
KV_DIM = ATT_KV_HEADS * HEAD_DIM
WINDOW = 128
BLOCK = 128
ATT_SCALE = HEAD_DIM ** -0.5
ROPE_BASE = 10000.0
KV_OFF = 2 * CONV_DIM + ATT_DIM
MIX_EVEN = CONV_DIM + ATT_DIM
RET_HEADS = 4
RET_DK = 256
RET_DV = 512
RET_QK = RET_HEADS * RET_DK
RET_V = RET_HEADS * RET_DV
D_FF = 2816
DEEPNORM_ALPHA = (2 * DEPTH) ** 0.25
LN_EPS = 1e-5
NEG_INF = -1e30

LANES = 128
SUBLANES = 8
MOD_ROWS = 16
VMEM_LIMIT_BYTES = 56 * 1024 * 1024
FFN_CHUNK = 256


def _params(*sem):
    return pltpu.CompilerParams(dimension_semantics=sem, vmem_limit_bytes=VMEM_LIMIT_BYTES)


def _dot(a, b):
    return jnp.dot(a, b, preferred_element_type=F32)


def _dot_nt(a, b):
    return lax.dot_general(a, b, (((1,), (1,)), ((), ())), preferred_element_type=F32)


def _dot_tn(a, b):
    return lax.dot_general(a, b, (((0,), (0,)), ((), ())), preferred_element_type=F32)


def _silu(x):
    return x * jax.nn.sigmoid(x)


def _standardize(x):
    mu = jnp.mean(x, axis=-1, keepdims=True)
    d = x - mu
    var = jnp.mean(d * d, axis=-1, keepdims=True)
    return d * lax.rsqrt(var + LN_EPS)


def _post_norm(h, out, g, b):
    return _standardize(DEEPNORM_ALPHA * h + out) * g + b


def _resident(shape, index_map):
    return pl.BlockSpec(shape, index_map, pipeline_mode=pl.Buffered(1))


def _mod_kernel(c_ref, w_ref, b_ref, o_ref):
    s = _silu(c_ref[...]).astype(BF16)
    o_ref[0] = _dot(s, w_ref[0].astype(BF16)) + b_ref[0]


def _modulation(cvec, w_ada, b_ada):
    tn = 1536
    n_out = 9 * D_MODEL
    out = pl.pallas_call(
        _mod_kernel,
        grid=(DEPTH, n_out // tn),
        in_specs=[
            pl.BlockSpec((MOD_ROWS, D_MODEL), lambda i, n: (0, 0)),
            pl.BlockSpec((1, D_MODEL, tn), lambda i, n: (i, 0, n)),
            pl.BlockSpec((1, 1, tn), lambda i, n: (i, 0, n)),
        ],
        out_specs=pl.BlockSpec((1, MOD_ROWS, tn), lambda i, n: (i, 0, n)),
        out_shape=jax.ShapeDtypeStruct((DEPTH, MOD_ROWS, n_out), F32),
        compiler_params=_params("parallel", "parallel"),
        name="adaln_modulation",
    )(cvec, w_ada, b_ada.reshape(DEPTH, 1, n_out))
    return out.reshape(DEPTH, MOD_ROWS, 9, D_MODEL)


def _mod_spec(layer, row_of_batch):
    return pl.BlockSpec((None, None, 9, D_MODEL), lambda b, t: (layer, row_of_batch(b), 0, 0))


def _ln_spec(layer, j):
    return pl.BlockSpec((None, 1, D_MODEL), lambda b, t: (layer * 3 + j, 0, 0))


def _ffn_kernel(h_ref, mod_ref, win_ref, wout_ref, g_ref, b_ref, o_ref, act_ref, *, k0):
    h = h_ref[...]
    shift, scale, gate = mod_ref[k0:k0 + 1], mod_ref[k0 + 1:k0 + 2], mod_ref[k0 + 2:k0 + 3]
    u = (h * (1.0 + scale) + shift).astype(BF16)
    for n in range(D_FF // FFN_CHUNK):
        lo = n * FFN_CHUNK
        gt = _dot(u, win_ref[:, lo:lo + FFN_CHUNK])
        up = _dot(u, win_ref[:, D_FF + lo:D_FF + lo + FFN_CHUNK])
        act_ref[:, lo:lo + FFN_CHUNK] = (_silu(gt) * up).astype(BF16)
    y = _dot(act_ref[...], wout_ref[...])
    o_ref[...] = _post_norm(h, (0.5 * gate) * y, g_ref[...], b_ref[...])


def _ffn_half(h, mod, row_of_batch, w_in, w_out, ln_g, ln_b, layer, j, k0, tm):
    bsz, n, _ = h.shape
    return pl.pallas_call(
        functools.partial(_ffn_kernel, k0=k0),
        grid=(bsz, n // tm),
        in_specs=[
            pl.BlockSpec((None, tm, D_MODEL), lambda b, t: (b, t, 0)),
            _mod_spec(layer, row_of_batch),
            _resident((None, None, D_MODEL, 2 * D_FF), lambda b, t: (layer, j, 0, 0)),
            _resident((None, None, D_FF, D_MODEL), lambda b, t: (layer, j, 0, 0)),
            _ln_spec(layer, 2 * j),
            _ln_spec(layer, 2 * j),
        ],
        out_specs=pl.BlockSpec((None, tm, D_MODEL), lambda b, t: (b, t, 0)),
        out_shape=jax.ShapeDtypeStruct(h.shape, F32),
        scratch_shapes=[pltpu.VMEM((tm, D_FF), BF16)],
        compiler_params=_params("parallel", "parallel"),
        name="ffn_half",
    )(h, mod, w_in, w_out, ln_g, ln_b)


EVEN_COLS = 2 * CONV_DIM + ATT_DIM + 4 * KV_DIM


def _rope64(x, cos, sin):
    width = x.shape[-1]
    reps = width // LANES
    cosw = jnp.concatenate([cos] * reps, axis=1) if reps > 1 else cos
    sinw = jnp.concatenate([sin] * reps, axis=1) if reps > 1 else sin
    lane = lax.broadcasted_iota(jnp.int32, x.shape, 1)
    first_half = (lane % HEAD_DIM) < (HEAD_DIM // 2)
    partner = jnp.where(first_half, pltpu.roll(x, width - HEAD_DIM // 2, 1), pltpu.roll(x, HEAD_DIM // 2, 1))
    return x * cosw + partner * sinw


CONV_ROWS = 64
CONV_SHIFT_PAD = (CONV_HALO - CONV_WIDTH // 2 + CONV_WIDTH - 1) // SUBLANES * SUBLANES
CONV_GROUP = 256
LOG2E = 1.4426950408889634


def _conv_taps(ybuf_ref, ysh_ref, conv_ref, cw_ref, cb_ref, cols, tm):
    for r in range(1, SUBLANES):
        ysh_ref[r - 1, :, cols] = ybuf_ref[r:r + tm + CONV_SHIFT_PAD, cols]
    first = CONV_HALO - CONV_WIDTH // 2
    for c in range(cols.start // LANES, cols.stop // LANES):
        cl = slice(c * LANES, (c + 1) * LANES)
        for r0 in range(0, tm, CONV_ROWS):
            acc = jnp.zeros((CONV_ROWS, LANES), F32) + cb_ref[:, cl]
            for j in range(CONV_WIDTH):
                a, r = divmod(first + j, SUBLANES)
                lo = r0 + a * SUBLANES
                window = ysh_ref[r - 1, lo:lo + CONV_ROWS, cl] if r else ybuf_ref[lo:lo + CONV_ROWS, cl]
                acc = acc + window * cw_ref[j:j + 1, cl]
            conv_ref[r0:r0 + CONV_ROWS, cl] = acc


def _even_proj_kernel(*refs, rope, tm, n_tiles):
    refs = list(refs)
    hprev_ref, h_ref, hnext_ref, mod_ref, w_ref = refs[:5]
    pos = 5
    if rope:
        cos_ref, sin_ref = refs[pos:pos + 2]
        pos += 2
    (cw_ref, cb_ref, cg_ref, cbeta_ref, cq_ref, kv_ref, ybuf_ref, ysh_ref, conv_ref) = refs[pos:]
    t = pl.program_id(1)

    def modulated(x):
        return (x * (1.0 + mod_ref[4:5]) + mod_ref[3:4]).astype(BF16)

    u = modulated(h_ref[...])
    u_ext = jnp.concatenate([modulated(hprev_ref[...]), u, modulated(hnext_ref[...])], axis=0)
    erow = lax.broadcasted_iota(jnp.int32, (tm + 2 * CONV_HALO, 1), 0)
    inside = (erow >= jnp.where(t > 0, 0, CONV_HALO)) & (erow < jnp.where(t < n_tiles - 1, tm + 2 * CONV_HALO, tm + CONV_HALO))
    for g0 in range(0, CONV_DIM, CONV_GROUP):
        cols = slice(g0, g0 + CONV_GROUP)
        a = _dot(u_ext, w_ref[:, g0:g0 + CONV_GROUP])
        gt = _dot(u_ext, w_ref[:, CONV_DIM + g0:CONV_DIM + g0 + CONV_GROUP])
        ybuf_ref[:, cols] = jnp.where(inside, a * jax.nn.sigmoid(gt), 0.0)
        _conv_taps(ybuf_ref, ysh_ref, conv_ref, cw_ref, cb_ref, cols, tm)
    cq_ref[:, 0:CONV_DIM] = _silu(_standardize(conv_ref[...]) * cg_ref[...] + cbeta_ref[...]).astype(BF16)
    q = _dot(u, w_ref[:, 2 * CONV_DIM:KV_OFF])
    k = _dot(u, w_ref[:, KV_OFF:KV_OFF + 2 * KV_DIM])
    v = _dot(u, w_ref[:, KV_OFF + 2 * KV_DIM:EVEN_COLS])
    if rope:
        cos, sin = cos_ref[...], sin_ref[...]
        q = _rope64(q, cos, sin)
        k = _rope64(k, cos, sin)
    cq_ref[:, CONV_DIM:] = (q * (ATT_SCALE * LOG2E)).astype(BF16)
    kv_ref[:, 0:2 * KV_DIM] = k.astype(BF16)
    kv_ref[:, 2 * KV_DIM:] = v.astype(BF16)


def _even_proj(h, mod, row_of_batch, w, cw, cb, cg, cbeta, layer, j, rope_tables, tm):
    bsz, n, _ = h.shape
    n_tiles = n // tm
    hb = tm // CONV_HALO
    rope = rope_tables is not None
    small = lambda rows: pl.BlockSpec((None, rows, CONV_DIM), lambda b, t: (j, 0, 0))
    in_specs = [
        pl.BlockSpec((None, CONV_HALO, D_MODEL), lambda b, t: (b, jnp.maximum(t * hb - 1, 0), 0)),
        pl.BlockSpec((None, tm, D_MODEL), lambda b, t: (b, t, 0)),
        pl.BlockSpec((None, CONV_HALO, D_MODEL), lambda b, t: (b, jnp.minimum((t + 1) * hb, n // CONV_HALO - 1), 0)),
        _mod_spec(layer, row_of_batch),
        _resident((None, D_MODEL, EVEN_COLS), lambda b, t: (j, 0, 0)),
    ]
    args = [h, h, h, mod, w]
    if rope:
        in_specs += [pl.BlockSpec((tm, LANES), lambda b, t: (t, 0))] * 2
        args += list(rope_tables)
    in_specs += [small(CONV_WIDTH), small(1), small(1), small(1)]
    args += [cw, cb, cg, cbeta]

    def out(width):
        return (pl.BlockSpec((None, tm, width), lambda b, t: (b, t, 0)),
                jax.ShapeDtypeStruct((bsz, n, width), BF16))

    outs = [out(MIX_EVEN), out(4 * KV_DIM)]
    return pl.pallas_call(
        functools.partial(_even_proj_kernel, rope=rope, tm=tm, n_tiles=n_tiles),
        grid=(bsz, n_tiles),
        in_specs=in_specs,
        out_specs=[o[0] for o in outs],
        out_shape=[o[1] for o in outs],
        scratch_shapes=[pltpu.VMEM((tm + 2 * CONV_HALO, CONV_DIM), F32),
                        pltpu.VMEM((SUBLANES - 1, tm + CONV_SHIFT_PAD, CONV_DIM), F32),
                        pltpu.VMEM((tm, CONV_DIM), F32)],
        compiler_params=_params("parallel", "parallel"),
        name="even_proj",
    )(*args)


def _attend(qe, kloc, vloc, kctx, vctx, sink_col, valid):
    s_ctx = _dot_nt(qe, kctx)
    m = jnp.maximum(jnp.max(s_ctx, axis=-1, keepdims=True), sink_col)
    if kloc is not None:
        s_loc = jnp.where(valid, _dot_nt(qe, kloc), NEG_INF)
        m = jnp.maximum(m, jnp.max(s_loc, axis=-1, keepdims=True))
        p_loc = jnp.exp2(s_loc - m)
    p_ctx = jnp.exp2(s_ctx - m)
    den = jnp.sum(p_ctx, axis=-1, keepdims=True) + jnp.exp2(sink_col - m)
    o = _dot(p_ctx.astype(BF16), vctx)
    if kloc is not None:
        den = den + jnp.sum(p_loc, axis=-1, keepdims=True)
        o = o + _dot(p_loc.astype(BF16), vloc)
    return o / den


def _even_mix_kernel(*refs, latent, tq, n_tiles):
    if latent:
        (sink_ref, h_ref, mod_ref, cq_ref, kvprev_ref, kvcur_ref, kvnext_ref, kvctx_ref,
         wout_ref, g_ref, b_ref, o_ref, kvs_ref, cat_ref) = refs
    else:
        (sink_ref, h_ref, mod_ref, cq_ref, kvctx_ref, wout_ref, g_ref, b_ref, o_ref, cat_ref) = refs
    i = pl.program_id(1)
    if latent:
        kvs_ref[0:BLOCK] = kvprev_ref[...]
        kvs_ref[BLOCK:BLOCK + tq] = kvcur_ref[...]
        kvs_ref[BLOCK + tq:] = kvnext_ref[...]
    cat_ref[:, 0:CONV_DIM] = cq_ref[:, 0:CONV_DIM]

    rows = 4 * BLOCK
    lane = lax.broadcasted_iota(jnp.int32, (2 * BLOCK, LANES), 1)
    row = lax.broadcasted_iota(jnp.int32, (rows, 1), 0)
    qrow = lax.broadcasted_iota(jnp.int32, (rows, 3 * BLOCK), 0) % BLOCK
    kcol = lax.broadcasted_iota(jnp.int32, (rows, 3 * BLOCK), 1)
    band = jnp.abs(kcol - BLOCK - qrow) <= WINDOW
    for n in range(tq // BLOCK):
        valid = None
        if latent:
            lo = 0 if n > 0 else jnp.where(i > 0, 0, BLOCK)
            hi = 3 * BLOCK if n < tq // BLOCK - 1 else jnp.where(i < n_tiles - 1, 3 * BLOCK, 2 * BLOCK)
            valid = band & (kcol >= lo) & (kcol < hi)
        for hk in range(ATT_KV_HEADS):
            kvl = slice(hk * LANES, (hk + 1) * LANES)
            vvl = slice(2 * KV_DIM + hk * LANES, 2 * KV_DIM + (hk + 1) * LANES)
            q2 = jnp.concatenate(
                [cq_ref[n * BLOCK:(n + 1) * BLOCK, CONV_DIM + (2 * hk + p) * LANES:CONV_DIM + (2 * hk + p + 1) * LANES]
                 for p in range(2)], axis=0)
            kloc = vloc = None
            if latent:
                kloc = kvs_ref[n * BLOCK:(n + 3) * BLOCK, kvl]
                vloc = kvs_ref[n * BLOCK:(n + 3) * BLOCK, vvl]
            zero = jnp.zeros_like(q2)
            q4 = jnp.concatenate([jnp.where(lane < HEAD_DIM, q2, zero), jnp.where(lane >= HEAD_DIM, q2, zero)], axis=0)
            sinks = [sink_ref[hk * 4 + 2 * p + e] * LOG2E for e in range(2) for p in range(2)]
            sink_col = jnp.where(row < 2 * BLOCK, jnp.where(row < BLOCK, sinks[0], sinks[1]),
                                 jnp.where(row < 3 * BLOCK, sinks[2], sinks[3]))
            o4 = _attend(q4, kloc, vloc, kvctx_ref[:, kvl], kvctx_ref[:, vvl], sink_col, valid)
            o2 = jnp.where(lane < HEAD_DIM, o4[:2 * BLOCK], o4[2 * BLOCK:]).astype(BF16)
            for p in range(2):
                c0 = CONV_DIM + (2 * hk + p) * LANES
                cat_ref[n * BLOCK:(n + 1) * BLOCK, c0:c0 + LANES] = o2[p * BLOCK:(p + 1) * BLOCK]
    y = _dot(cat_ref[...], wout_ref[...])
    o_ref[...] = _post_norm(h_ref[...], mod_ref[5:6] * y, g_ref[...], b_ref[...])


def _even_mix(h, mod, row_of_batch, cq, kv, kvctx, sink, w_out, ln_g, ln_b, layer, j, latent, tq):
    bsz, n, _ = h.shape
    n_tiles = n // tq
    kb = tq // BLOCK

    def tile(width):
        return pl.BlockSpec((None, tq, width), lambda b, t: (b, t, 0))

    def prev(rows, per_tile, width):
        return pl.BlockSpec((None, rows, width), lambda b, t: (b, jnp.maximum(t * per_tile - 1, 0), 0))

    def nxt(rows, per_tile, width):
        last = n // rows - 1
        return pl.BlockSpec((None, rows, width), lambda b, t: (b, jnp.minimum((t + 1) * per_tile, last), 0))

    in_specs = [pl.BlockSpec(memory_space=pltpu.SMEM), tile(D_MODEL), _mod_spec(layer, row_of_batch), tile(MIX_EVEN)]
    args = [sink, h, mod, cq]
    if latent:
        in_specs += [prev(BLOCK, kb, 4 * KV_DIM), tile(4 * KV_DIM), nxt(BLOCK, kb, 4 * KV_DIM)]
        args += [kv, kv, kv]
    in_specs += [pl.BlockSpec((None, CTX_LEN, 4 * KV_DIM), lambda b, t: (b, 0, 0)),
                 _resident((None, MIX_EVEN, D_MODEL), lambda b, t: (j, 0, 0)),
                 _ln_spec(layer, 1), _ln_spec(layer, 1)]
    args += [kvctx, w_out, ln_g, ln_b]
    scratch = [pltpu.VMEM((tq + 2 * BLOCK, 4 * KV_DIM), BF16)] if latent else []
    scratch += [pltpu.VMEM((tq, MIX_EVEN), BF16)]
    return pl.pallas_call(
        functools.partial(_even_mix_kernel, latent=latent, tq=tq, n_tiles=n_tiles),
        grid=(bsz, n_tiles),
        in_specs=in_specs,
        out_specs=tile(D_MODEL),
        out_shape=jax.ShapeDtypeStruct(h.shape, F32),
        scratch_shapes=scratch,
        compiler_params=_params("parallel", "parallel"),
        name="even_mix",
    )(*args)


RET_COLS = 2 * RET_QK + 2 * RET_V


def _rope256(x, cos, sin):
    half = RET_DK // 2
    parts = []
    for hd in range(RET_HEADS):
        x1 = x[:, hd * RET_DK:hd * RET_DK + half]
        x2 = x[:, hd * RET_DK + half:(hd + 1) * RET_DK]
        parts += [x1 * cos - x2 * sin, x2 * cos + x1 * sin]
    return jnp.concatenate(parts, axis=1)


def _ret_proj_kernel(*refs, rope):
    if rope:
        h_ref, mod_ref, w_ref, cos_ref, sin_ref, qkv_ref, sg_ref = refs
    else:
        h_ref, mod_ref, w_ref, qkv_ref, sg_ref = refs
    u = (h_ref[...] * (1.0 + mod_ref[4:5]) + mod_ref[3:4]).astype(BF16)
    q = _dot(u, w_ref[:, 0:RET_QK])
    k = _dot(u, w_ref[:, RET_QK:2 * RET_QK])
    if rope:
        cos, sin = cos_ref[...], sin_ref[...]
        q = _rope256(q, cos, sin)
        k = _rope256(k, cos, sin)
    qkv_ref[:, 0:RET_QK] = (q * (RET_DK ** -0.5)).astype(BF16)
    qkv_ref[:, RET_QK:2 * RET_QK] = k.astype(BF16)
    qkv_ref[:, 2 * RET_QK:] = _dot(u, w_ref[:, 2 * RET_QK:2 * RET_QK + RET_V]).astype(BF16)
    sg_ref[...] = _silu(_dot(u, w_ref[:, 2 * RET_QK + RET_V:RET_COLS])).astype(BF16)


def _ret_proj(h, mod, row_of_batch, w, layer, j, rope_tables, tm):
    bsz, n, _ = h.shape
    rope = rope_tables is not None
    in_specs = [
        pl.BlockSpec((None, tm, D_MODEL), lambda b, t: (b, t, 0)),
        _mod_spec(layer, row_of_batch),
        _resident((None, D_MODEL, RET_COLS), lambda b, t: (j, 0, 0)),
    ]
    args = [h, mod, w]
    if rope:
        in_specs += [pl.BlockSpec((tm, LANES), lambda b, t: (t, 0))] * 2
        args += list(rope_tables)
    widths = [2 * RET_QK + RET_V, RET_V]
    return pl.pallas_call(
        functools.partial(_ret_proj_kernel, rope=rope),
        grid=(bsz, n // tm),
        in_specs=in_specs,
        out_specs=[pl.BlockSpec((None, tm, wd), lambda b, t: (b, t, 0)) for wd in widths],
        out_shape=[jax.ShapeDtypeStruct((bsz, n, wd), BF16) for wd in widths],
        compiler_params=_params("parallel", "parallel"),
        name="ret_proj",
    )(*args)


RET_BLOCK = 256
RET_STEP_ROWS = 512


def _ret_mix_kernel(*refs, nc, step, has_init, out_state):
    refs = list(refs)
    decay_ref, qkv_ref, sg_ref, h_ref, mod_ref = refs[:5]
    pos = 5
    init_ref = fin_ref = None
    if has_init:
        init_ref = refs[pos]
        pos += 1
    w_ref, g_ref, b_ref, o_ref = refs[pos:pos + 4]
    pos += 4
    if out_state:
        fin_ref = refs[pos]
        pos += 1
    state_ref, dmat_ref, qdec_ref, kdec_ref, cdec_ref, ob_ref, z_ref = refs[pos:]
    s = pl.program_id(1)
    rb = RET_BLOCK

    def load_state(d):
        for hd in range(RET_HEADS):
            state_ref[hd] = init_ref[hd] if has_init else jnp.zeros((RET_DK, RET_DV), F32)

    @pl.when(s == 0)
    def _():
        prow = lax.broadcasted_iota(jnp.int32, (rb, rb), 0)
        pcol = lax.broadcasted_iota(jnp.int32, (rb, rb), 1)
        posf = lax.broadcasted_iota(jnp.int32, (rb, LANES), 0).astype(F32)
        for d in range(2):
            rel = (pcol - prow) if d else (prow - pcol)
            keep = (rel > 0) if d else (rel >= 0)
            relf = jnp.where(keep, rel, 0).astype(F32)
            qpow = (rb - posf) if d else (posf + 1.0)
            kpow = posf if d else (rb - 1.0 - posf)
            for hd in range(RET_HEADS):
                def log_gamma(shape):
                    return -jnp.log(1.0 + jnp.exp(-jnp.full(shape, decay_ref[d, hd], F32)))

                dmat_ref[d, hd] = jnp.where(keep, jnp.exp(log_gamma((rb, rb)) * relf), 0.0)
                qdec_ref[d, hd] = jnp.exp(log_gamma((rb, LANES)) * qpow)
                kdec_ref[d, hd] = jnp.exp(log_gamma((rb, LANES)) * kpow)
                cdec_ref[d, hd] = jnp.exp(log_gamma((8, LANES)) * float(rb))
        load_state(1)

    @pl.when(s == nc)
    def _():
        if out_state:
            fin_ref[1] = state_ref[...]
        load_state(0)

    def lanes(t, width):
        return jnp.concatenate([t] * (width // LANES), axis=1)

    def head_out(d, hd, r0):
        rows = slice(r0, r0 + rb)
        qh = qkv_ref[rows, hd * RET_DK:(hd + 1) * RET_DK]
        kh = qkv_ref[rows, RET_QK + hd * RET_DK:RET_QK + (hd + 1) * RET_DK]
        vh = qkv_ref[rows, 2 * RET_QK + hd * RET_DV:2 * RET_QK + (hd + 1) * RET_DV]
        st = state_ref[hd]
        sc = _dot_nt(qh, kh) * dmat_ref[d, hd]
        o = _dot(sc.astype(BF16), vh) + _dot((qh * lanes(qdec_ref[d, hd], RET_DK)).astype(BF16), st.astype(BF16))
        kd = (kh * lanes(kdec_ref[d, hd], RET_DK)).astype(BF16)
        state_ref[hd] = st * lanes(cdec_ref[d, hd, 0:1, :], RET_DV) + _dot_tn(kd, vh)
        return o

    @pl.when(s < nc)
    def _():
        row0 = pl.multiple_of((nc - 1 - s) * step, step)
        for r0 in reversed(range(0, step, rb)):
            for hd in range(RET_HEADS):
                ob_ref[pl.ds(row0 + r0, rb), hd * RET_DV:(hd + 1) * RET_DV] = head_out(1, hd, r0).astype(BF16)

    @pl.when(s >= nc)
    def _():
        row0 = pl.multiple_of((s - nc) * step, step)
        for r0 in range(0, step, rb):
            for hd in range(RET_HEADS):
                cols = slice(hd * RET_DV, (hd + 1) * RET_DV)
                on = _standardize(head_out(0, hd, r0) + ob_ref[pl.ds(row0 + r0, rb), cols].astype(F32))
                z_ref[r0:r0 + rb, cols] = (sg_ref[r0:r0 + rb, cols] * on).astype(BF16)
        y = _dot(z_ref[...], w_ref[...])
        o_ref[...] = _post_norm(h_ref[...], mod_ref[5:6] * y, g_ref[...], b_ref[...])

    if out_state:
        @pl.when(s == 2 * nc - 1)
        def _():
            fin_ref[0] = state_ref[...]


def _ret_mix(decay, qkv, sg, h, mod, row_of_batch, init, w, ln_g, ln_b, layer, j, out_state):
    bsz, n, _ = h.shape
    rb = RET_BLOCK
    step = min(RET_STEP_ROWS, n)
    nc = n // step
    chunk = lambda b, s: (b, jnp.where(s < nc, nc - 1 - s, s - nc), 0)
    fwd = lambda b, s: (b, jnp.maximum(s - nc, 0), 0)
    state_shape = (None, 2, RET_HEADS, RET_DK, RET_DV)
    state_index = lambda b, s: (b, 0, 0, 0, 0)
    in_specs = [pl.BlockSpec(memory_space=pltpu.SMEM),
                pl.BlockSpec((None, step, 2 * RET_QK + RET_V), chunk),
                pl.BlockSpec((None, step, RET_V), fwd),
                pl.BlockSpec((None, step, D_MODEL), fwd),
                _mod_spec(layer, row_of_batch)]
    args = [decay, qkv, sg, h, mod]
    if init is not None:
        in_specs.append(_resident((None, None, RET_HEADS, RET_DK, RET_DV),
                                  lambda b, s: (b, jnp.where(s < nc, 1, 0), 0, 0, 0)))
        args.append(init)
    in_specs += [_resident((None, RET_V, D_MODEL), lambda b, s: (j, 0, 0)), _ln_spec(layer, 1), _ln_spec(layer, 1)]
    args += [w, ln_g, ln_b]
    out_specs = [pl.BlockSpec((None, step, D_MODEL), fwd)]
    out_shape = [jax.ShapeDtypeStruct(h.shape, F32)]
    if out_state:
        out_specs.append(pl.BlockSpec(state_shape, state_index))
        out_shape.append(jax.ShapeDtypeStruct((bsz, 2, RET_HEADS, RET_DK, RET_DV), F32))
    return pl.pallas_call(
        functools.partial(_ret_mix_kernel, nc=nc, step=step, has_init=init is not None, out_state=out_state),
        grid=(bsz, 2 * nc),
        in_specs=in_specs,
        out_specs=out_specs,
        out_shape=out_shape,
        scratch_shapes=[pltpu.VMEM((RET_HEADS, RET_DK, RET_DV), F32),
                        pltpu.VMEM((2, RET_HEADS, rb, rb), F32),
                        pltpu.VMEM((2, RET_HEADS, rb, LANES), F32),
                        pltpu.VMEM((2, RET_HEADS, rb, LANES), F32),
                        pltpu.VMEM((2, RET_HEADS, 8, LANES), F32),
                        pltpu.VMEM((n, RET_V), BF16),
                        pltpu.VMEM((step, RET_V), BF16)],
        compiler_params=_params("parallel", "arbitrary"),
        name="ret_mix",
    )(*args)


def _axial_rope_tables(n_tokens):
    rows = n_tokens // GRID_W
    r = jnp.broadcast_to(jnp.arange(rows, dtype=F32)[:, None], (rows, GRID_W)).reshape(-1)
    cidx = jnp.broadcast_to(jnp.arange(GRID_W, dtype=F32)[None, :], (rows, GRID_W)).reshape(-1)
    nf = HEAD_DIM // 4
    inv = ROPE_BASE ** (-jnp.arange(nf, dtype=F32) / nf)
    ang = jnp.concatenate([r[:, None] * inv[None], cidx[:, None] * inv[None]], axis=-1)
    cos, sin = jnp.cos(ang), jnp.sin(ang)
    reps = LANES // HEAD_DIM
    return jnp.tile(jnp.concatenate([cos, cos], axis=-1), (1, reps)), jnp.tile(jnp.concatenate([-sin, sin], axis=-1), (1, reps))


def _retention_rope_tables(n_tokens):
    inv = 1.0 / (ROPE_BASE ** jnp.linspace(0.0, 1.0, RET_DK // 2, dtype=F32))
    ang = jnp.arange(n_tokens, dtype=F32)[:, None] * inv[None]
    return jnp.cos(ang), jnp.sin(ang)


def _even_weight_layout(ev_w_in):
    def dup(cols):
        heads = [cols[..., hd * HEAD_DIM:(hd + 1) * HEAD_DIM] for hd in range(ATT_KV_HEADS)]
        return jnp.concatenate([t for hd in heads for t in (hd, hd)], axis=-1)

    return jnp.concatenate([ev_w_in[..., :KV_OFF], dup(ev_w_in[..., KV_OFF:KV_OFF + KV_DIM]),
                            dup(ev_w_in[..., KV_OFF + KV_DIM:])], axis=-1).astype(BF16)


def kernel(x, c, ctx, c_ctx, w_ada, b_ada, ln_g, ln_b, ffn_w_in, ffn_w_out, ev_w_in, ev_conv_w, ev_conv_b,
           ev_norm_g, ev_norm_b, ev_sink, ev_w_out, ret_w_in, ret_decay, ret_w_out):
    bsz, n_tok, _ = x.shape
    assert bsz < MOD_ROWS and ctx.shape[1] == CTX_LEN
    rope_a = _axial_rope_tables(n_tok)
    rope_r = _retention_rope_tables(n_tok)
    cvec = jnp.zeros((MOD_ROWS, D_MODEL), F32).at[:bsz].set(c).at[bsz].set(c_ctx)
    mod = _modulation(cvec, w_ada, b_ada)
    lat_row = lambda b: b
    ctx_row = lambda b: bsz

    ffn_in = ffn_w_in.astype(BF16)
    ffn_out = ffn_w_out.astype(BF16)
    ev_in = _even_weight_layout(ev_w_in)
    ev_out = ev_w_out.astype(BF16)
    ret_in = ret_w_in.astype(BF16)
    ret_out = ret_w_out.astype(BF16)
    lng = ln_g.reshape(DEPTH * 3, 1, D_MODEL)
    lnb = ln_b.reshape(DEPTH * 3, 1, D_MODEL)
    conv_b = ev_conv_b[:, None, :]
    conv_g = ev_norm_g[:, None, :]
    conv_beta = ev_norm_b[:, None, :]

    tm_lat = min(512, n_tok)
    tm_ctx = CTX_LEN
    tq_lat = min(512, n_tok)

    def ffn(hh, row, tm, layer, j):
        return _ffn_half(hh, mod, row, ffn_in, ffn_out, lng, lnb, layer, j, 6 * j, tm)

    def ffn_ctx(hh, layer, j):
        flat = hh.reshape(1, bsz * CTX_LEN, D_MODEL)
        return ffn(flat, ctx_row, min(tm_lat, bsz * CTX_LEN), layer, j).reshape(hh.shape)

    h, hc = x, ctx
    for i in range(DEPTH):
        last = i == DEPTH - 1
        j = i // 2
        h = ffn(h, lat_row, tm_lat, i, 0)
        hc = ffn_ctx(hc, i, 0)
        if i % 2 == 0:
            proj = functools.partial(_even_proj, w=ev_in, cw=ev_conv_w, cb=conv_b, cg=conv_g, cbeta=conv_beta,
                                     layer=i, j=j)
            cqc, kvc = proj(hc, mod, ctx_row, rope_tables=None, tm=tm_ctx)
            cql, kvl = proj(h, mod, lat_row, rope_tables=rope_a, tm=tm_lat)
            mix = functools.partial(_even_mix, kvctx=kvc, sink=ev_sink[j], w_out=ev_out, ln_g=lng, ln_b=lnb,
                                    layer=i, j=j)
            h = mix(h, mod, lat_row, cql, kvl, latent=True, tq=tq_lat)
            if not last:
                hc = mix(hc, mod, ctx_row, cqc, None, latent=False, tq=CTX_LEN)
        else:
            decay = ret_decay[j]
            flat = hc.reshape(1, bsz * CTX_LEN, D_MODEL)
            qkvc, sgc = [t.reshape(bsz, CTX_LEN, -1) for t in _ret_proj(
                flat, mod, ctx_row, ret_in, i, j, None, min(tm_lat, bsz * CTX_LEN))]
            qkvl, sgl = _ret_proj(h, mod, lat_row, ret_in, i, j, rope_r, tm_lat)
            hc_next, states = _ret_mix(decay, qkvc, sgc, hc, mod, ctx_row, None, ret_out, lng, lnb, i, j, True)
            h, = _ret_mix(decay, qkvl, sgl, h, mod, lat_row, states, ret_out, lng, lnb, i, j, False)
            if not last:
                hc = hc_next
        h = ffn(h, lat_row, tm_lat, i, 1)
        if not last:
            hc = ffn_ctx(hc, i, 1)
    return h
```

```python
import functools

import jax
import jax.numpy as jnp
from jax import lax
from jax.experimental import pallas as pl
from jax.experimental.pallas import tpu as pltpu

F32 = jnp.float32
BF16 = jnp.bfloat16

D_MODEL = 1024
DEPTH = 4
GRID_W = 64
CTX_LEN = 256
CONV_DIM = 512
CONV_WIDTH = 31
CONV_HALO = 16
ATT_HEADS = 8
ATT_KV_HEADS = 2
HEAD_DIM = 64
ATT_DIM = ATT_HEADS * HEAD_DIM
KV_DIM = ATT_KV_HEADS * HEAD_DIM
WINDOW = 128
BLOCK = 128
ATT_SCALE = HEAD_DIM ** -0.5
ROPE_BASE = 10000.0
KV_OFF = 2 * CONV_DIM + ATT_DIM
MIX_EVEN = CONV_DIM + ATT_DIM
RET_HEADS = 4
RET_DK = 256
RET_DV = 512
RET_QK = RET_HEADS * RET_DK
RET_V = RET_HEADS * RET_DV
D_FF = 2816
DEEPNORM_ALPHA = (2 * DEPTH) ** 0.25
LN_EPS = 1e-5
NEG_INF = -1e30

LANES = 128
SUBLANES = 8
MOD_ROWS = 16
VMEM_LIMIT_BYTES = 56 * 1024 * 1024
FFN_CHUNK = 256


def _params(*sem):
    return pltpu.CompilerParams(dimension_semantics=sem, vmem_limit_bytes=VMEM_LIMIT_BYTES)


def _dot(a, b):
    return jnp.dot(a, b, preferred_element_type=F32)


def _dot_nt(a, b):
    return lax.dot_general(a, b, (((1,), (1,)), ((), ())), preferred_element_type=F32)


def _dot_tn(a, b):
    return lax.dot_general(a, b, (((0,), (0,)), ((), ())), preferred_element_type=F32)


def _silu(x):
    return x * jax.nn.sigmoid(x)


def _standardize(x):
    mu = jnp.mean(x, axis=-1, keepdims=True)
    d = x - mu
    var = jnp.mean(d * d, axis=-1, keepdims=True)
    return d * lax.rsqrt(var + LN_EPS)


def _post_norm(h, out, g, b):
    return _standardize(DEEPNORM_ALPHA * h + out) * g + b


def _resident(shape, index_map):
    return pl.BlockSpec(shape, index_map, pipeline_mode=pl.Buffered(1))


def _mod_kernel(c_ref, w_ref, b_ref, o_ref):
    s = _silu(c_ref[...]).astype(BF16)
    o_ref[0] = _dot(s, w_ref[0].astype(BF16)) + b_ref[0]


def _modulation(cvec, w_ada, b_ada):
    tn = 1536
    n_out = 9 * D_MODEL
    out = pl.pallas_call(
        _mod_kernel,
        grid=(DEPTH, n_out // tn),
        in_specs=[
            pl.BlockSpec((MOD_ROWS, D_MODEL), lambda i, n: (0, 0)),
            pl.BlockSpec((1, D_MODEL, tn), lambda i, n: (i, 0, n)),
            pl.BlockSpec((1, 1, tn), lambda i, n: (i, 0, n)),
        ],
        out_specs=pl.BlockSpec((1, MOD_ROWS, tn), lambda i, n: (i, 0, n)),
        out_shape=jax.ShapeDtypeStruct((DEPTH, MOD_ROWS, n_out), F32),
        compiler_params=_params("parallel", "parallel"),
        name="adaln_modulation",
    )(cvec, w_ada, b_ada.reshape(DEPTH, 1, n_out))
    return out.reshape(DEPTH, MOD_ROWS, 9, D_MODEL)


def _mod_spec(layer, row_of_batch):
    return pl.BlockSpec((None, None, 9, D_MODEL), lambda b, t: (layer, row_of_batch(b), 0, 0))


def _ln_spec(layer, j):
    return pl.BlockSpec((None, 1, D_MODEL), lambda b, t: (layer * 3 + j, 0, 0))


def _ffn_kernel(h_ref, mod_ref, win_ref, wout_ref, g_ref, b_ref, o_ref, act_ref, *, k0):
    h = h_ref[...]
    shift, scale, gate = mod_ref[k0:k0 + 1], mod_ref[k0 + 1:k0 + 2], mod_ref[k0 + 2:k0 + 3]
    u = (h * (1.0 + scale) + shift).astype(BF16)
    for n in range(D_FF // FFN_CHUNK):
        lo = n * FFN_CHUNK
        gt = _dot(u, win_ref[:, lo:lo + FFN_CHUNK])
        up = _dot(u, win_ref[:, D_FF + lo:D_FF + lo + FFN_CHUNK])
        act_ref[:, lo:lo + FFN_CHUNK] = (_silu(gt) * up).astype(BF16)
    y = _dot(act_ref[...], wout_ref[...])
    o_ref[...] = _post_norm(h, (0.5 * gate) * y, g_ref[...], b_ref[...])


def _ffn_half(h, mod, row_of_batch, w_in, w_out, ln_g, ln_b, layer, j, k0, tm):
    bsz, n, _ = h.shape
    return pl.pallas_call(
        functools.partial(_ffn_kernel, k0=k0),
        grid=(bsz, n // tm),
        in_specs=[
            pl.BlockSpec((None, tm, D_MODEL), lambda b, t: (b, t, 0)),
            _mod_spec(layer, row_of_batch),
            _resident((None, None, D_MODEL, 2 * D_FF), lambda b, t: (layer, j, 0, 0)),
            _resident((None, None, D_FF, D_MODEL), lambda b, t: (layer, j, 0, 0)),
            _ln_spec(layer, 2 * j),
            _ln_spec(layer, 2 * j),
        ],
        out_specs=pl.BlockSpec((None, tm, D_MODEL), lambda b, t: (b, t, 0)),
        out_shape=jax.ShapeDtypeStruct(h.shape, F32),
        scratch_shapes=[pltpu.VMEM((tm, D_FF), BF16)],
        compiler_params=_params("parallel", "parallel"),
        name="ffn_half",
    )(h, mod, w_in, w_out, ln_g, ln_b)


EVEN_COLS = 2 * CONV_DIM + ATT_DIM + 4 * KV_DIM


def _rope64(x, cos, sin):
    width = x.shape[-1]
    reps = width // LANES
    cosw = jnp.concatenate([cos] * reps, axis=1) if reps > 1 else cos
    sinw = jnp.concatenate([sin] * reps, axis=1) if reps > 1 else sin
    lane = lax.broadcasted_iota(jnp.int32, x.shape, 1)
    first_half = (lane % HEAD_DIM) < (HEAD_DIM // 2)
    partner = jnp.where(first_half, pltpu.roll(x, width - HEAD_DIM // 2, 1), pltpu.roll(x, HEAD_DIM // 2, 1))
    return x * cosw + partner * sinw


CONV_ROWS = 64
CONV_SHIFT_PAD = (CONV_HALO - CONV_WIDTH // 2 + CONV_WIDTH - 1) // SUBLANES * SUBLANES
CONV_GROUP = 256
LOG2E = 1.4426950408889634


def _conv_taps(ybuf_ref, ysh_ref, conv_ref, cw_ref, cb_ref, cols, tm):
    for r in range(1, SUBLANES):
        ysh_ref[r - 1, :, cols] = ybuf_ref[r:r + tm + CONV_SHIFT_PAD, cols]
    first = CONV_HALO - CONV_WIDTH // 2
    for c in range(cols.start // LANES, cols.stop // LANES):
        cl = slice(c * LANES, (c + 1) * LANES)
        for r0 in range(0, tm, CONV_ROWS):
            acc = jnp.zeros((CONV_ROWS, LANES), F32) + cb_ref[:, cl]
            for j in range(CONV_WIDTH):
                a, r = divmod(first + j, SUBLANES)
                lo = r0 + a * SUBLANES
                window = ysh_ref[r - 1, lo:lo + CONV_ROWS, cl] if r else ybuf_ref[lo:lo + CONV_ROWS, cl]
                acc = acc + window * cw_ref[j:j + 1, cl]
            conv_ref[r0:r0 + CONV_ROWS, cl] = acc


def _even_proj_kernel(*refs, rope, tm, n_tiles):
    refs = list(refs)
    hprev_ref, h_ref, hnext_ref, mod_ref, w_ref = refs[:5]
    pos = 5
    if rope:
        cos_ref, sin_ref = refs[pos:pos + 2]
        pos += 2
    (cw_ref, cb_ref, cg_ref, cbeta_ref, cq_ref, kv_ref, ybuf_ref, ysh_ref, conv_ref) = refs[pos:]
    t = pl.program_id(1)

    def modulated(x):
        return (x * (1.0 + mod_ref[4:5]) + mod_ref[3:4]).astype(BF16)

    u = modulated(h_ref[...])
    u_ext = jnp.concatenate([modulated(hprev_ref[...]), u, modulated(hnext_ref[...])], axis=0)
    erow = lax.broadcasted_iota(jnp.int32, (tm + 2 * CONV_HALO, 1), 0)
    inside = (erow >= jnp.where(t > 0, 0, CONV_HALO)) & (erow < jnp.where(t < n_tiles - 1, tm + 2 * CONV_HALO, tm + CONV_HALO))
    for g0 in range(0, CONV_DIM, CONV_GROUP):
        cols = slice(g0, g0 + CONV_GROUP)
        a = _dot(u_ext, w_ref[:, g0:g0 + CONV_GROUP])
        gt = _dot(u_ext, w_ref[:, CONV_DIM + g0:CONV_DIM + g0 + CONV_GROUP])
        ybuf_ref[:, cols] = jnp.where(inside, a * jax.nn.sigmoid(gt), 0.0)
        _conv_taps(ybuf_ref, ysh_ref, conv_ref, cw_ref, cb_ref, cols, tm)
    cq_ref[:, 0:CONV_DIM] = _silu(_standardize(conv_ref[...]) * cg_ref[...] + cbeta_ref[...]).astype(BF16)
    q = _dot(u, w_ref[:, 2 * CONV_DIM:KV_OFF])
    k = _dot(u, w_ref[:, KV_OFF:KV_OFF + 2 * KV_DIM])
    v = _dot(u, w_ref[:, KV_OFF + 2 * KV_DIM:EVEN_COLS])
    if rope:
        cos, sin = cos_ref[...], sin_ref[...]
        q = _rope64(q, cos, sin)
        k = _rope64(k, cos, sin)
    cq_ref[:, CONV_DIM:] = (q * (ATT_SCALE * LOG2E)).astype(BF16)
    kv_ref[:, 0:2 * KV_DIM] = k.astype(BF16)
    kv_ref[:, 2 * KV_DIM:] = v.astype(BF16)


def _even_proj(h, mod, row_of_batch, w, cw, cb, cg, cbeta, layer, j, rope_tables, tm):
    bsz, n, _ = h.shape
    n_tiles = n // tm
    hb = tm // CONV_HALO
    rope = rope_tables is not None
    small = lambda rows: pl.BlockSpec((None, rows, CONV_DIM), lambda b, t: (j, 0, 0))
    in_specs = [
        pl.BlockSpec((None, CONV_HALO, D_MODEL), lambda b, t: (b, jnp.maximum(t * hb - 1, 0), 0)),
        pl.BlockSpec((None, tm, D_MODEL), lambda b, t: (b, t, 0)),
        pl.BlockSpec((None, CONV_HALO, D_MODEL), lambda b, t: (b, jnp.minimum((t + 1) * hb, n // CONV_HALO - 1), 0)),
        _mod_spec(layer, row_of_batch),
        _resident((None, D_MODEL, EVEN_COLS), lambda b, t: (j, 0, 0)),
    ]
    args = [h, h, h, mod, w]
    if rope:
        in_specs += [pl.BlockSpec((tm, LANES), lambda b, t: (t, 0))] * 2
        args += list(rope_tables)
    in_specs += [small(CONV_WIDTH), small(1), small(1), small(1)]
    args += [cw, cb, cg, cbeta]

    def out(width):
        return (pl.BlockSpec((None, tm, width), lambda b, t: (b, t, 0)),
                jax.ShapeDtypeStruct((bsz, n, width), BF16))

    outs = [out(MIX_EVEN), out(4 * KV_DIM)]
    return pl.pallas_call(
        functools.partial(_even_proj_kernel, rope=rope, tm=tm, n_tiles=n_tiles),
        grid=(bsz, n_tiles),
        in_specs=in_specs,
        out_specs=[o[0] for o in outs],
        out_shape=[o[1] for o in outs],
        scratch_shapes=[pltpu.VMEM((tm + 2 * CONV_HALO, CONV_DIM), F32),
                        pltpu.VMEM((SUBLANES - 1, tm + CONV_SHIFT_PAD, CONV_DIM), F32),
                        pltpu.VMEM((tm, CONV_DIM), F32)],
        compiler_params=_params("parallel", "parallel"),
        name="even_proj",
    )(*args)


def _attend(qe, kloc, vloc, kctx, vctx, sink_col, valid):
    s_ctx = _dot_nt(qe, kctx)
    m = jnp.maximum(jnp.max(s_ctx, axis=-1, keepdims=True), sink_col)
    if kloc is not None:
        s_loc = jnp.where(valid, _dot_nt(qe, kloc), NEG_INF)
        m = jnp.maximum(m, jnp.max(s_loc, axis=-1, keepdims=True))
        p_loc = jnp.exp2(s_loc - m)
    p_ctx = jnp.exp2(s_ctx - m)
    den = jnp.sum(p_ctx, axis=-1, keepdims=True) + jnp.exp2(sink_col - m)
    o = _dot(p_ctx.astype(BF16), vctx)
    if kloc is not None:
        den = den + jnp.sum(p_loc, axis=-1, keepdims=True)
        o = o + _dot(p_loc.astype(BF16), vloc)
    return o / den


def _even_mix_kernel(*refs, latent, tq, n_tiles):
    if latent:
        (sink_ref, h_ref, mod_ref, cq_ref, kvprev_ref, kvcur_ref, kvnext_ref, kvctx_ref,
         wout_ref, g_ref, b_ref, o_ref, kvs_ref, cat_ref) = refs
    else:
        (sink_ref, h_ref, mod_ref, cq_ref, kvctx_ref, wout_ref, g_ref, b_ref, o_ref, cat_ref) = refs
    i = pl.program_id(1)
    if latent:
        kvs_ref[0:BLOCK] = kvprev_ref[...]
        kvs_ref[BLOCK:BLOCK + tq] = kvcur_ref[...]
        kvs_ref[BLOCK + tq:] = kvnext_ref[...]
    cat_ref[:, 0:CONV_DIM] = cq_ref[:, 0:CONV_DIM]

    rows = 4 * BLOCK
    lane = lax.broadcasted_iota(jnp.int32, (2 * BLOCK, LANES), 1)
    row = lax.broadcasted_iota(jnp.int32, (rows, 1), 0)
    qrow = lax.broadcasted_iota(jnp.int32, (rows, 3 * BLOCK), 0) % BLOCK
    kcol = lax.broadcasted_iota(jnp.int32, (rows, 3 * BLOCK), 1)
    band = jnp.abs(kcol - BLOCK - qrow) <= WINDOW
    for n in range(tq // BLOCK):
        valid = None
        if latent:
            lo = 0 if n > 0 else jnp.where(i > 0, 0, BLOCK)
            hi = 3 * BLOCK if n < tq // BLOCK - 1 else jnp.where(i < n_tiles - 1, 3 * BLOCK, 2 * BLOCK)
            valid = band & (kcol >= lo) & (kcol < hi)
        for hk in range(ATT_KV_HEADS):
            kvl = slice(hk * LANES, (hk + 1) * LANES)
            vvl = slice(2 * KV_DIM + hk * LANES, 2 * KV_DIM + (hk + 1) * LANES)
            q2 = jnp.concatenate(
                [cq_ref[n * BLOCK:(n + 1) * BLOCK, CONV_DIM + (2 * hk + p) * LANES:CONV_DIM + (2 * hk + p + 1) * LANES]
                 for p in range(2)], axis=0)
            kloc = vloc = None
            if latent:
                kloc = kvs_ref[n * BLOCK:(n + 3) * BLOCK, kvl]
                vloc = kvs_ref[n * BLOCK:(n + 3) * BLOCK, vvl]
            zero = jnp.zeros_like(q2)
            q4 = jnp.concatenate([jnp.where(lane < HEAD_DIM, q2, zero), jnp.where(lane >= HEAD_DIM, q2, zero)], axis=0)
            sinks = [sink_ref[hk * 4 + 2 * p + e] * LOG2E for e in range(2) for p in range(2)]
            sink_col = jnp.where(row < 2 * BLOCK, jnp.where(row < BLOCK, sinks[0], sinks[1]),
                                 jnp.where(row < 3 * BLOCK, sinks[2], sinks[3]))
            o4 = _attend(q4, kloc, vloc, kvctx_ref[:, kvl], kvctx_ref[:, vvl], sink_col, valid)
            o2 = jnp.where(lane < HEAD_DIM, o4[:2 * BLOCK], o4[2 * BLOCK:]).astype(BF16)
            for p in range(2):
                c0 = CONV_DIM + (2 * hk + p) * LANES
                cat_ref[n * BLOCK:(n + 1) * BLOCK, c0:c0 + LANES] = o2[p * BLOCK:(p + 1) * BLOCK]
    y = _dot(cat_ref[...], wout_ref[...])
    o_ref[...] = _post_norm(h_ref[...], mod_ref[5:6] * y, g_ref[...], b_ref[...])


def _even_mix(h, mod, row_of_batch, cq, kv, kvctx, sink, w_out, ln_g, ln_b, layer, j, latent, tq):
    bsz, n, _ = h.shape
    n_tiles = n // tq
    kb = tq // BLOCK

    def tile(width):
        return pl.BlockSpec((None, tq, width), lambda b, t: (b, t, 0))

    def prev(rows, per_tile, width):
        return pl.BlockSpec((None, rows, width), lambda b, t: (b, jnp.maximum(t * per_tile - 1, 0), 0))

    def nxt(rows, per_tile, width):
        last = n // rows - 1
        return pl.BlockSpec((None, rows, width), lambda b, t: (b, jnp.minimum((t + 1) * per_tile, last), 0))

    in_specs = [pl.BlockSpec(memory_space=pltpu.SMEM), tile(D_MODEL), _mod_spec(layer, row_of_batch), tile(MIX_EVEN)]
    args = [sink, h, mod, cq]
    if latent:
        in_specs += [prev(BLOCK, kb, 4 * KV_DIM), tile(4 * KV_DIM), nxt(BLOCK, kb, 4 * KV_DIM)]
        args += [kv, kv, kv]
    in_specs += [pl.BlockSpec((None, CTX_LEN, 4 * KV_DIM), lambda b, t: (b, 0, 0)),
                 _resident((None, MIX_EVEN, D_MODEL), lambda b, t: (j, 0, 0)),
                 _ln_spec(layer, 1), _ln_spec(layer, 1)]
    args += [kvctx, w_out, ln_g, ln_b]
    scratch = [pltpu.VMEM((tq + 2 * BLOCK, 4 * KV_DIM), BF16)] if latent else []
    scratch += [pltpu.VMEM((tq, MIX_EVEN), BF16)]
    return pl.pallas_call(
        functools.partial(_even_mix_kernel, latent=latent, tq=tq, n_tiles=n_tiles),
        grid=(bsz, n_tiles),
        in_specs=in_specs,
        out_specs=tile(D_MODEL),
        out_shape=jax.ShapeDtypeStruct(h.shape, F32),
        scratch_shapes=scratch,
        compiler_params=_params("parallel", "parallel"),
        name="even_mix",
    )(*args)


RET_COLS = 2 * RET_QK + 2 * RET_V


def _rope256(x, cos, sin):
    half = RET_DK // 2
    parts = []
    for hd in range(RET_HEADS):
        x1 = x[:, hd * RET_DK:hd * RET_DK + half]
        x2 = x[:, hd * RET_DK + half:(hd + 1) * RET_DK]
        parts += [x1 * cos - x2 * sin, x2 * cos + x1 * sin]
    return jnp.concatenate(parts, axis=1)


def _ret_proj_kernel(*refs, rope):
    if rope:
        h_ref, mod_ref, w_ref, cos_ref, sin_ref, qkv_ref, sg_ref = refs
    else:
        h_ref, mod_ref, w_ref, qkv_ref, sg_ref = refs
    u = (h_ref[...] * (1.0 + mod_ref[4:5]) + mod_ref[3:4]).astype(BF16)
    q = _dot(u, w_ref[:, 0:RET_QK])
    k = _dot(u, w_ref[:, RET_QK:2 * RET_QK])
    if rope:
        cos, sin = cos_ref[...], sin_ref[...]
        q = _rope256(q, cos, sin)
        k = _rope256(k, cos, sin)
    qkv_ref[:, 0:RET_QK] = (q * (RET_DK ** -0.5)).astype(BF16)
    qkv_ref[:, RET_QK:2 * RET_QK] = k.astype(BF16)
    qkv_ref[:, 2 * RET_QK:] = _dot(u, w_ref[:, 2 * RET_QK:2 * RET_QK + RET_V]).astype(BF16)
    sg_ref[...] = _silu(_dot(u, w_ref[:, 2 * RET_QK + RET_V:RET_COLS])).astype(BF16)


def _ret_proj(h, mod, row_of_batch, w, layer, j, rope_tables, tm):
    bsz, n, _ = h.shape
    rope = rope_tables is not None
    in_specs = [
        pl.BlockSpec((None, tm, D_MODEL), lambda b, t: (b, t, 0)),
        _mod_spec(layer, row_of_batch),
        _resident((None, D_MODEL, RET_COLS), lambda b, t: (j, 0, 0)),
    ]
    args = [h, mod, w]
    if rope:
        in_specs += [pl.BlockSpec((tm, LANES), lambda b, t: (t, 0))] * 2
        args += list(rope_tables)
    widths = [2 * RET_QK + RET_V, RET_V]
    return pl.pallas_call(
        functools.partial(_ret_proj_kernel, rope=rope),
        grid=(bsz, n // tm),
        in_specs=in_specs,
        out_specs=[pl.BlockSpec((None, tm, wd), lambda b, t: (b, t, 0)) for wd in widths],
        out_shape=[jax.ShapeDtypeStruct((bsz, n, wd), BF16) for wd in widths],
        compiler_params=_params("parallel", "parallel"),
        name="ret_proj",
    )(*args)


RET_BLOCK = 256
RET_STEP_ROWS = 512


def _ret_mix_kernel(*refs, nc, step, has_init, out_state):
    refs = list(refs)
    decay_ref, qkv_ref, sg_ref, h_ref, mod_ref = refs[:5]
    pos = 5
    init_ref = fin_ref = None
    if has_init:
        init_ref = refs[pos]
        pos += 1
    w_ref, g_ref, b_ref, o_ref = refs[pos:pos + 4]
    pos += 4
    if out_state:
        fin_ref = refs[pos]
        pos += 1
    state_ref, dmat_ref, qdec_ref, kdec_ref, cdec_ref, ob_ref, z_ref = refs[pos:]
    s = pl.program_id(1)
    rb = RET_BLOCK

    def load_state(d):
        for hd in range(RET_HEADS):
            state_ref[hd] = init_ref[hd] if has_init else jnp.zeros((RET_DK, RET_DV), F32)

    @pl.when(s == 0)
    def _():
        prow = lax.broadcasted_iota(jnp.int32, (rb, rb), 0)
        pcol = lax.broadcasted_iota(jnp.int32, (rb, rb), 1)
        posf = lax.broadcasted_iota(jnp.int32, (rb, LANES), 0).astype(F32)
        for d in range(2):
            rel = (pcol - prow) if d else (prow - pcol)
            keep = (rel > 0) if d else (rel >= 0)
            relf = jnp.where(keep, rel, 0).astype(F32)
            qpow = (rb - posf) if d else (posf + 1.0)
            kpow = posf if d else (rb - 1.0 - posf)
            for hd in range(RET_HEADS):
                def log_gamma(shape):
                    return -jnp.log(1.0 + jnp.exp(-jnp.full(shape, decay_ref[d, hd], F32)))

                dmat_ref[d, hd] = jnp.where(keep, jnp.exp(log_gamma((rb, rb)) * relf), 0.0)
                qdec_ref[d, hd] = jnp.exp(log_gamma((rb, LANES)) * qpow)
                kdec_ref[d, hd] = jnp.exp(log_gamma((rb, LANES)) * kpow)
                cdec_ref[d, hd] = jnp.exp(log_gamma((8, LANES)) * float(rb))
        load_state(1)

    @pl.when(s == nc)
    def _():
        if out_state:
            fin_ref[1] = state_ref[...]
        load_state(0)

    def lanes(t, width):
        return jnp.concatenate([t] * (width // LANES), axis=1)

    def head_out(d, hd, r0):
        rows = slice(r0, r0 + rb)
        qh = qkv_ref[rows, hd * RET_DK:(hd + 1) * RET_DK]
        kh = qkv_ref[rows, RET_QK + hd * RET_DK:RET_QK + (hd + 1) * RET_DK]
        vh = qkv_ref[rows, 2 * RET_QK + hd * RET_DV:2 * RET_QK + (hd + 1) * RET_DV]
        st = state_ref[hd]
        sc = _dot_nt(qh, kh) * dmat_ref[d, hd]
        o = _dot(sc.astype(BF16), vh) + _dot((qh * lanes(qdec_ref[d, hd], RET_DK)).astype(BF16), st.astype(BF16))
        kd = (kh * lanes(kdec_ref[d, hd], RET_DK)).astype(BF16)
        state_ref[hd] = st * lanes(cdec_ref[d, hd, 0:1, :], RET_DV) + _dot_tn(kd, vh)
        return o

    @pl.when(s < nc)
    def _():
        row0 = pl.multiple_of((nc - 1 - s) * step, step)
        for r0 in reversed(range(0, step, rb)):
            for hd in range(RET_HEADS):
                ob_ref[pl.ds(row0 + r0, rb), hd * RET_DV:(hd + 1) * RET_DV] = head_out(1, hd, r0).astype(BF16)

    @pl.when(s >= nc)
    def _():
        row0 = pl.multiple_of((s - nc) * step, step)
        for r0 in range(0, step, rb):
            for hd in range(RET_HEADS):
                cols = slice(hd * RET_DV, (hd + 1) * RET_DV)
                on = _standardize(head_out(0, hd, r0) + ob_ref[pl.ds(row0 + r0, rb), cols].astype(F32))
                z_ref[r0:r0 + rb, cols] = (sg_ref[r0:r0 + rb, cols] * on).astype(BF16)
        y = _dot(z_ref[...], w_ref[...])
        o_ref[...] = _post_norm(h_ref[...], mod_ref[5:6] * y, g_ref[...], b_ref[...])

    if out_state:
        @pl.when(s == 2 * nc - 1)
        def _():
            fin_ref[0] = state_ref[...]


def _ret_mix(decay, qkv, sg, h, mod, row_of_batch, init, w, ln_g, ln_b, layer, j, out_state):
    bsz, n, _ = h.shape
    rb = RET_BLOCK
    step = min(RET_STEP_ROWS, n)
    nc = n // step
    chunk = lambda b, s: (b, jnp.where(s < nc, nc - 1 - s, s - nc), 0)
    fwd = lambda b, s: (b, jnp.maximum(s - nc, 0), 0)
    state_shape = (None, 2, RET_HEADS, RET_DK, RET_DV)
    state_index = lambda b, s: (b, 0, 0, 0, 0)
    in_specs = [pl.BlockSpec(memory_space=pltpu.SMEM),
                pl.BlockSpec((None, step, 2 * RET_QK + RET_V), chunk),
                pl.BlockSpec((None, step, RET_V), fwd),
                pl.BlockSpec((None, step, D_MODEL), fwd),
                _mod_spec(layer, row_of_batch)]
    args = [decay, qkv, sg, h, mod]
    if init is not None:
        in_specs.append(_resident((None, None, RET_HEADS, RET_DK, RET_DV),
                                  lambda b, s: (b, jnp.where(s < nc, 1, 0), 0, 0, 0)))
        args.append(init)
    in_specs += [_resident((None, RET_V, D_MODEL), lambda b, s: (j, 0, 0)), _ln_spec(layer, 1), _ln_spec(layer, 1)]
    args += [w, ln_g, ln_b]
    out_specs = [pl.BlockSpec((None, step, D_MODEL), fwd)]
    out_shape = [jax.ShapeDtypeStruct(h.shape, F32)]
    if out_state:
        out_specs.append(pl.BlockSpec(state_shape, state_index))
        out_shape.append(jax.ShapeDtypeStruct((bsz, 2, RET_HEADS, RET_DK, RET_DV), F32))
    return pl.pallas_call(
        functools.partial(_ret_mix_kernel, nc=nc, step=step, has_init=init is not None, out_state=out_state),
        grid=(bsz, 2 * nc),
        in_specs=in_specs,
        out_specs=out_specs,
        out_shape=out_shape,
        scratch_shapes=[pltpu.VMEM((RET_HEADS, RET_DK, RET_DV), F32),
                        pltpu.VMEM((2, RET_HEADS, rb, rb), F32),
                        pltpu.VMEM((2, RET_HEADS, rb, LANES), F32),
                        pltpu.VMEM((2, RET_HEADS, rb, LANES), F32),
                        pltpu.VMEM((2, RET_HEADS, 8, LANES), F32),
                        pltpu.VMEM((n, RET_V), BF16),
                        pltpu.VMEM((step, RET_V), BF16)],
        compiler_params=_params("parallel", "arbitrary"),
        name="ret_mix",
    )(*args)


def _axial_rope_tables(n_tokens):
    rows = n_tokens // GRID_W
    r = jnp.broadcast_to(jnp.arange(rows, dtype=F32)[:, None], (rows, GRID_W)).reshape(-1)
    cidx = jnp.broadcast_to(jnp.arange(GRID_W, dtype=F32)[None, :], (rows, GRID_W)).reshape(-1)
    nf = HEAD_DIM // 4
    inv = ROPE_BASE ** (-jnp.arange(nf, dtype=F32) / nf)
    ang = jnp.concatenate([r[:, None] * inv[None], cidx[:, None] * inv[None]], axis=-1)
    cos, sin = jnp.cos(ang), jnp.sin(ang)
    reps = LANES // HEAD_DIM
    return jnp.tile(jnp.concatenate([cos, cos], axis=-1), (1, reps)), jnp.tile(jnp.concatenate([-sin, sin], axis=-1), (1, reps))


def _retention_rope_tables(n_tokens):
    inv = 1.0 / (ROPE_BASE ** jnp.linspace(0.0, 1.0, RET_DK // 2, dtype=F32))
    ang = jnp.arange(n_tokens, dtype=F32)[:, None] * inv[None]
    return jnp.cos(ang), jnp.sin(ang)


def _even_weight_layout(ev_w_in):
    def dup(cols):
        heads = [cols[..., hd * HEAD_DIM:(hd + 1) * HEAD_DIM] for hd in range(ATT_KV_HEADS)]
        return jnp.concatenate([t for hd in heads for t in (hd, hd)], axis=-1)

    return jnp.concatenate([ev_w_in[..., :KV_OFF], dup(ev_w_in[..., KV_OFF:KV_OFF + KV_DIM]),
                            dup(ev_w_in[..., KV_OFF + KV_DIM:])], axis=-1).astype(BF16)


def kernel(x, c, ctx, c_ctx, w_ada, b_ada, ln_g, ln_b, ffn_w_in, ffn_w_out, ev_w_in, ev_conv_w, ev_conv_b,
           ev_norm_g, ev_norm_b, ev_sink, ev_w_out, ret_w_in, ret_decay, ret_w_out):
    bsz, n_tok, _ = x.shape
    assert bsz < MOD_ROWS and ctx.shape[1] == CTX_LEN
    rope_a = _axial_rope_tables(n_tok)
    rope_r = _retention_rope_tables(n_tok)
    cvec = jnp.zeros((MOD_ROWS, D_MODEL), F32).at[:bsz].set(c).at[bsz].set(c_ctx)
    mod = _modulation(cvec, w_ada, b_ada)
    lat_row = lambda b: b
    ctx_row = lambda b: bsz

    ffn_in = ffn_w_in.astype(BF16)
    ffn_out = ffn_w_out.astype(BF16)
    ev_in = _even_weight_layout(ev_w_in)
    ev_out = ev_w_out.astype(BF16)
    ret_in = ret_w_in.astype(BF16)
    ret_out = ret_w_out.astype(BF16)
    lng = ln_g.reshape(DEPTH * 3, 1, D_MODEL)
    lnb = ln_b.reshape(DEPTH * 3, 1, D_MODEL)
    conv_b = ev_conv_b[:, None, :]
    conv_g = ev_norm_g[:, None, :]
    conv_beta = ev_norm_b[:, None, :]

    tm_lat = min(512, n_tok)
    tm_ctx = CTX_LEN
    tq_lat = min(1024, n_tok)

    def ffn(hh, row, tm, layer, j):
        return _ffn_half(hh, mod, row, ffn_in, ffn_out, lng, lnb, layer, j, 6 * j, tm)

    def ffn_ctx(hh, layer, j):
        flat = hh.reshape(1, bsz * CTX_LEN, D_MODEL)
        return ffn(flat, ctx_row, min(tm_lat, bsz * CTX_LEN), layer, j).reshape(hh.shape)

    h, hc = x, ctx
    for i in range(DEPTH):
        last = i == DEPTH - 1
        j = i // 2
        h = ffn(h, lat_row, tm_lat, i, 0)
        hc = ffn_ctx(hc, i, 0)
        if i % 2 == 0:
            proj = functools.partial(_even_proj, w=ev_in, cw=ev_conv_w, cb=conv_b, cg=conv_g, cbeta=conv_beta,
                                     layer=i, j=j)
            cqc, kvc = proj(hc, mod, ctx_row, rope_tables=None, tm=tm_ctx)
            cql, kvl = proj(h, mod, lat_row, rope_tables=rope_a, tm=tm_lat)
            mix = functools.partial(_even_mix, kvctx=kvc, sink=ev_sink[j], w_out=ev_out, ln_g=lng, ln_b=lnb,
                                    layer=i, j=j)
            h = mix(h, mod, lat_row, cql, kvl, latent=True, tq=tq_lat)
            if not last:
                hc = mix(hc, mod, ctx_row, cqc, None, latent=False, tq=CTX_LEN)
        else:
            decay = ret_decay[j]
            flat = hc.reshape(1, bsz * CTX_LEN, D_MODEL)
            qkvc, sgc = [t.reshape(bsz, CTX_LEN, -1) for t in _ret_proj(
                flat, mod, ctx_row, ret_in, i, j, None, min(tm_lat, bsz * CTX_LEN))]
            qkvl, sgl = _ret_proj(h, mod, lat_row, ret_in, i, j, rope_r, tm_lat)
            hc_next, states = _ret_mix(decay, qkvc, sgc, hc, mod, ctx_row, None, ret_out, lng, lnb, i, j, True)
            h, = _ret_mix(decay, qkvl, sgl, h, mod, lat_row, states, ret_out, lng, lnb, i, j, False)
            if not last:
                hc = hc_next
        h = ffn(h, lat_row, tm_lat, i, 1)
        if not last:
            hc = ffn_ctx(hc, i, 1)
    return h
```
